```python
import math
import jax, jax.numpy as jnp
from jax import lax
import numpy as np

D_MODEL = 1024
BATCH = 1
SEQ = 16384
DEPTH = 2
DEC_BATCH = 32
DEC_SEQ = 8
PAST_LEN = 16384
PAGE_SIZE = 128

RET_HEADS = 4
RET_QK_DIM = 128
RET_V_DIM = 256
RET_QK_W = RET_HEADS * RET_QK_DIM
RET_V_W = RET_HEADS * RET_V_DIM
RET_CHUNK = 128
ROPE_BASE = 10000.0
DIFF_HEADS = 8
DIFF_HEAD_DIM = 64
DIFF_QK_W = DIFF_HEADS * 2 * DIFF_HEAD_DIM
DIFF_V_DIM = 2 * DIFF_HEAD_DIM
DIFF_V_W = DIFF_HEADS * DIFF_V_DIM
Q_BLOCK = 128
LRU_W = D_MODEL
LRU_BLOCKS = 4
LRU_BLOCK_W = LRU_W // LRU_BLOCKS
CONV_W = 4
LRU_C = 8.0
N_BRANCH = 3
BRANCH_W = D_MODEL
D_FF = ((8 * D_MODEL + 3 * 256 - 1) // (3 * 256)) * 256
IN_SIZES = (RET_QK_W, RET_QK_W, RET_V_W, RET_V_W, DIFF_QK_W, DIFF_QK_W, DIFF_V_W, LRU_W, N_BRANCH * D_MODEL)
IN_W = 2 * RET_QK_W + 2 * RET_V_W + 2 * DIFF_QK_W + DIFF_V_W + LRU_W + N_BRANCH * D_MODEL
EPS = 1e-6

kernel_name = "hybrid_ret_diffattn_rglru_step"


def _split_points():
    pts, acc = [], 0
    for s in IN_SIZES[:-1]:
        acc += s
        pts.append(acc)
    return pts


def rms_norm(x, g):
    xf = x.astype(jnp.float32)
    y = xf * lax.rsqrt(jnp.mean(xf * xf, axis=-1, keepdims=True) + EPS)
    return (y * g.astype(jnp.float32)).astype(x.dtype)


def group_norm(o, g):
    of = o.astype(jnp.float32)
    mu = jnp.mean(of, axis=-1, keepdims=True)
    var = jnp.mean(jnp.square(of - mu), axis=-1, keepdims=True)
    return ((of - mu) * lax.rsqrt(var + EPS) * g.astype(jnp.float32)).astype(o.dtype)


def rotary(x, pos):
    half = x.shape[-1] // 2
    inv = ROPE_BASE ** (-jnp.arange(half, dtype=jnp.float32) / half)
    ang = pos.astype(jnp.float32)[:, None] * inv[None, :]
    cos = jnp.cos(ang)[None, :, None, :]
    sin = jnp.sin(ang)[None, :, None, :]
    xf = x.astype(jnp.float32)
    x1, x2 = xf[..., :half], xf[..., half:]
    return jnp.concatenate([x1 * cos - x2 * sin, x1 * sin + x2 * cos], axis=-1).astype(x.dtype)


def retention_chunkwise(q, k, v, s0):
    B, L, H, _ = q.shape
    dv = v.shape[-1]
    C = math.gcd(L, RET_CHUNK)
    n = L // C
    lg = jnp.log1p(-(2.0 ** (-5.0 - jnp.arange(H, dtype=jnp.float32))))
    idx = jnp.arange(C, dtype=jnp.float32)
    rel = idx[:, None] - idx[None, :]
    decay_mask = jnp.where(rel >= 0, jnp.exp(lg[:, None, None] * jnp.maximum(rel, 0.0)), 0.0)
    read_decay = jnp.exp((idx[:, None] + 1.0) * lg[None, :])
    write_decay = jnp.exp((C - 1.0 - idx)[:, None] * lg[None, :])
    chunk_decay = jnp.exp(C * lg)

    def to_chunks(t):
        return t.astype(jnp.float32).reshape(B, n, C, H, t.shape[-1]).swapaxes(0, 1)

    def step(s, inp):
        qc, kc, vc = inp
        sc = jnp.einsum('bihd,bjhd->bhij', qc, kc) * decay_mask
        o = (jnp.einsum('bhij,bjhe->bihe', sc, vc)
             + jnp.einsum('bihd,bhde->bihe', qc, s) * read_decay[None, :, :, None])
        s = (s * chunk_decay[None, :, None, None]
             + jnp.einsum('bjhd,bjhe->bhde', kc * write_decay[None, :, :, None], vc))
        return s, o

    s, o = lax.scan(step, s0.astype(jnp.float32), (to_chunks(q), to_chunks(k), to_chunks(v)))
    o = o.swapaxes(0, 1).reshape(B, L, H, dv)
    return o.astype(v.dtype), s.astype(s0.dtype)


def diff_core(q, k, v, q_pos, k_pos, lam):
    s = jnp.einsum('bqhcd,bkhcd->bhcqk', q, k).astype(jnp.float32) * (DIFF_HEAD_DIM ** -0.5)
    mask = k_pos[None, :] <= q_pos[:, None]
    p = jax.nn.softmax(jnp.where(mask, s, -jnp.inf), axis=-1)
    a = p[:, :, 0] - lam * p[:, :, 1]
    return jnp.einsum('bhqk,bkhe->bqhe', a.astype(v.dtype), v)


def diff_attn_prompt(q, k, v, lam):
    B, S = q.shape[:2]
    nb = S // Q_BLOCK
    qb = q.reshape(B, nb, Q_BLOCK, DIFF_HEADS, 2, DIFF_HEAD_DIM).swapaxes(0, 1)
    pos = jnp.arange(S)
    qpb = pos.reshape(nb, Q_BLOCK)
    out = lax.map(lambda a: diff_core(a[0], k, v, a[1], pos, lam), (qb, qpb))
    return out.swapaxes(0, 1).reshape(B, S, DIFF_HEADS, DIFF_V_DIM)


def diff_attn_sample(q, k_new, v_new, lam, cache_k, cache_v, page_table, l):
    L = q.shape[1]
    past = page_table.shape[1] * PAGE_SIZE
    k_pos = jnp.arange(past + L)
    q_pos = past + jnp.arange(L)

    def one(args):
        qi, ki, vi, pages = args
        kp = cache_k[l, pages].reshape(past, DIFF_HEADS, 2, DIFF_HEAD_DIM)
        vp = cache_v[l, pages].reshape(past, DIFF_HEADS, DIFF_V_DIM)
        kk = jnp.concatenate([kp, ki.astype(kp.dtype)], axis=0)[None]
        vv = jnp.concatenate([vp, vi.astype(vp.dtype)], axis=0)[None]
        return diff_core(qi[None], kk, vv, q_pos, k_pos, lam)[0]

    return lax.map(one, (q, k_new, v_new, page_table))


def causal_conv(x, buf, w, b):
    L = x.shape[1]
    xe = jnp.concatenate([buf.astype(x.dtype), x], axis=1)
    y = b
    for j in range(CONV_W):
        y = y + xe[:, j:j + L] * w[j]
    return y, xe[:, -(CONV_W - 1):]


def rg_lru(x, h0, wa, ba, wx, bx, lam):
    B, L, W = x.shape
    xf = x.astype(jnp.float32)
    xb = xf.reshape(B, L, LRU_BLOCKS, LRU_BLOCK_W)
    r = jax.nn.sigmoid(jnp.einsum('blni,nij->blnj', xb, wa.astype(jnp.float32)).reshape(B, L, W) + ba)
    i = jax.nn.sigmoid(jnp.einsum('blni,nij->blnj', xb, wx.astype(jnp.float32)).reshape(B, L, W) + bx)
    log_a = -LRU_C * r * jax.nn.softplus(-lam.astype(jnp.float32))
    a = jnp.exp(log_a)
    b = jnp.sqrt(-jnp.expm1(2.0 * log_a)) * (i * xf)
    b = b.at[:, 0].add(a[:, 0] * h0.astype(jnp.float32))

    def comb(e1, e2):
        a1, b1 = e1
        a2, b2 = e2
        return a1 * a2, a2 * b1 + b2

    _, h = lax.associative_scan(comb, (a, b), axis=1)
    return h.astype(x.dtype), h[:, -1].astype(h0.dtype)


def hybrid_layer(x, pos, attend, ret_s0, conv_buf, h0, p, lam_init):
    B, L, _ = x.shape
    h = rms_norm(x, p['attn_norm_g'])
    z = h @ p['w_in']
    rq, rk, rv, rg, dq, dk, dv, lx, gz = jnp.split(z, _split_points(), axis=-1)
    rq = rotary(rq.reshape(B, L, RET_HEADS, RET_QK_DIM), pos)
    rk = rotary(rk.reshape(B, L, RET_HEADS, RET_QK_DIM), pos) * (RET_QK_DIM ** -0.5)
    ro, ret_s = retention_chunkwise(rq, rk, rv.reshape(B, L, RET_HEADS, RET_V_DIM), ret_s0)
    ro = group_norm(ro, p['ret_gn_g']).reshape(B, L, RET_V_W) * jax.nn.silu(rg)
    lq1, lk1, lq2, lk2 = p['diff_lambda'].astype(jnp.float32)
    lam = jnp.exp(jnp.sum(lq1 * lk1)) - jnp.exp(jnp.sum(lq2 * lk2)) + lam_init
    dq = dq.reshape(B, L, DIFF_HEADS, 2, DIFF_HEAD_DIM)
    dk5 = dk.reshape(B, L, DIFF_HEADS, 2, DIFF_HEAD_DIM)
    dv4 = dv.reshape(B, L, DIFF_HEADS, DIFF_V_DIM)
    do = attend(dq, dk5, dv4, lam)
    do = (rms_norm(do, p['diff_ln_g']) * (1.0 - lam_init)).reshape(B, L, DIFF_V_W)
    xc, conv_new = causal_conv(lx, conv_buf, p['conv_w'], p['conv_b'])
    lo, h_last = rg_lru(xc, h0, p['gate_a_w'], p['gate_a_b'], p['gate_x_w'], p['gate_x_b'], p['lru_lambda'])
    g = jax.nn.sigmoid(gz.reshape(B, L, N_BRANCH, D_MODEL))
    br = jnp.stack([ro, do, lo], axis=2)
    proj = jnp.einsum('blnc,ncd->blnd', br, p['w_branch'])
    x = x + jnp.sum(g * proj, axis=2) @ p['w_out']
    h2 = rms_norm(x, p['ffn_norm_g'])
    x = x + (jax.nn.silu(h2 @ p['w_gate']) * (h2 @ p['w_up'])) @ p['w_down']
    new_k = dk5.reshape(B, L, DIFF_HEADS, 2 * DIFF_HEAD_DIM)
    return x, new_k, dv4, ret_s, conv_new, h_last


def setup_inputs(seed: int = 0) -> dict:
    key = jax.random.key(seed)
    ks = jax.random.split(key, 32)
    f32 = jnp.float32
    n_pages = PAST_LEN // PAGE_SIZE
    n_pool = (5 * DEC_BATCH * n_pages) // 4

    def nrm(k, shape, s):
        return jax.random.normal(k, shape, f32) * s

    u = jax.random.uniform(ks[19], (DEPTH, LRU_W), f32, 0.9, 0.999)
    sg = u ** (1.0 / LRU_C)
    return {
        'x_prompt': nrm(ks[0], (BATCH, SEQ, D_MODEL), 1.0),
        'x_sample': nrm(ks[1], (DEC_BATCH, DEC_SEQ, D_MODEL), 1.0),
        'cache_k': nrm(ks[2], (DEPTH, n_pool, PAGE_SIZE, DIFF_HEADS, 2 * DIFF_HEAD_DIM), 1.0),
        'cache_v': nrm(ks[3], (DEPTH, n_pool, PAGE_SIZE, DIFF_HEADS, DIFF_V_DIM), 1.0),
        'page_table': jax.random.permutation(ks[4], n_pool)[:DEC_BATCH * n_pages].reshape(DEC_BATCH, n_pages).astype(jnp.int32),
        'state_ret': nrm(ks[5], (DEPTH, DEC_BATCH, RET_HEADS, RET_QK_DIM, RET_V_DIM), 0.3),
        'state_conv': nrm(ks[6], (DEPTH, DEC_BATCH, CONV_W - 1, LRU_W), 1.0),
        'state_lru': nrm(ks[7], (DEPTH, DEC_BATCH, LRU_W), 0.5),
        'attn_norm_g': 1.0 + nrm(ks[8], (DEPTH, D_MODEL), 0.01),
        'w_in': nrm(ks[9], (DEPTH, D_MODEL, IN_W), D_MODEL ** -0.5),
        'ret_gn_g': 1.0 + nrm(ks[10], (DEPTH, RET_HEADS, RET_V_DIM), 0.01),
        'diff_lambda': nrm(ks[11], (DEPTH, 4, DIFF_HEAD_DIM), 0.1),
        'diff_ln_g': 1.0 + nrm(ks[12], (DEPTH, DIFF_V_DIM), 0.01),
        'conv_w': nrm(ks[13], (DEPTH, CONV_W, LRU_W), CONV_W ** -0.5),
        'conv_b': nrm(ks[14], (DEPTH, LRU_W), 0.01),
        'gate_a_w': nrm(ks[15], (DEPTH, LRU_BLOCKS, LRU_BLOCK_W, LRU_BLOCK_W), LRU_BLOCK_W ** -0.5),
        'gate_a_b': nrm(ks[16], (DEPTH, LRU_W), 0.01),
        'gate_x_w': nrm(ks[17], (DEPTH, LRU_BLOCKS, LRU_BLOCK_W, LRU_BLOCK_W), LRU_BLOCK_W ** -0.5),
        'gate_x_b': nrm(ks[18], (DEPTH, LRU_W), 0.01),
        'lru_lambda': jnp.log(sg) - jnp.log1p(-sg),
        'w_branch': nrm(ks[20], (DEPTH, N_BRANCH, BRANCH_W, D_MODEL), BRANCH_W ** -0.5),
        'w_out': nrm(ks[21], (DEPTH, D_MODEL, D_MODEL), D_MODEL ** -0.5),
        'ffn_norm_g': 1.0 + nrm(ks[22], (DEPTH, D_MODEL), 0.01),
        'w_gate': nrm(ks[23], (DEPTH, D_MODEL, D_FF), D_MODEL ** -0.5),
        'w_up': nrm(ks[24], (DEPTH, D_MODEL, D_FF), D_MODEL ** -0.5),
        'w_down': nrm(ks[25], (DEPTH, D_FF, D_MODEL), D_FF ** -0.5),
        'final_norm_g': 1.0 + nrm(ks[26], (D_MODEL,), 0.01),
    }


def reference(x_prompt, x_sample, cache_k, cache_v, page_table, state_ret, state_conv, state_lru,
              attn_norm_g, w_in, ret_gn_g, diff_lambda, diff_ln_g, conv_w, conv_b,
              gate_a_w, gate_a_b, gate_x_w, gate_x_b, lru_lambda, w_branch, w_out,
              ffn_norm_g, w_gate, w_up, w_down, final_norm_g):
    B, S, _ = x_prompt.shape
    L = x_sample.shape[1]
    past = page_table.shape[1] * PAGE_SIZE
    pos_p = jnp.arange(S)
    pos_s = past + jnp.arange(L)
    dt = x_prompt.dtype
    ret0 = jnp.zeros((B, RET_HEADS, RET_QK_DIM, RET_V_DIM), dt)
    conv0 = jnp.zeros((B, CONV_W - 1, LRU_W), dt)
    h00 = jnp.zeros((B, LRU_W), dt)

    xp, xs = x_prompt, x_sample
    kp_l, vp_l, rp_l, cp_l, hp_l = [], [], [], [], []
    ks_l, vs_l, rs_l, cs_l, hs_l = [], [], [], [], []
    for l in range(DEPTH):
        p = {'attn_norm_g': attn_norm_g[l], 'w_in': w_in[l], 'ret_gn_g': ret_gn_g[l],
             'diff_lambda': diff_lambda[l], 'diff_ln_g': diff_ln_g[l], 'conv_w': conv_w[l],
             'conv_b': conv_b[l], 'gate_a_w': gate_a_w[l], 'gate_a_b': gate_a_b[l],
             'gate_x_w': gate_x_w[l], 'gate_x_b': gate_x_b[l], 'lru_lambda': lru_lambda[l],
             'w_branch': w_branch[l], 'w_out': w_out[l], 'ffn_norm_g': ffn_norm_g[l],
             'w_gate': w_gate[l], 'w_up': w_up[l], 'w_down': w_down[l]}
        lam_init = 0.8 - 0.6 * math.exp(-0.3 * l)

        def attend_s(q, k, v, lam, l=l):
            return diff_attn_sample(q, k, v, lam, cache_k, cache_v, page_table, l)

        xp, k1, v1, r1, c1, h1 = hybrid_layer(xp, pos_p, diff_attn_prompt, ret0, conv0, h00, p, lam_init)
        xs, k2, v2, r2, c2, h2 = hybrid_layer(xs, pos_s, attend_s, state_ret[l], state_conv[l], state_lru[l], p, lam_init)
        kp_l.append(k1); vp_l.append(v1); rp_l.append(r1); cp_l.append(c1); hp_l.append(h1)
        ks_l.append(k2); vs_l.append(v2); rs_l.append(r2); cs_l.append(c2); hs_l.append(h2)

    y_prompt = rms_norm(xp, final_norm_g)
    y_sample = rms_norm(xs, final_norm_g)
    return (y_prompt, y_sample,
            jnp.stack(kp_l), jnp.stack(vp_l), jnp.stack(rp_l), jnp.stack(cp_l), jnp.stack(hp_l),
            jnp.stack(ks_l), jnp.stack(vs_l), jnp.stack(rs_l), jnp.stack(cs_l), jnp.stack(hs_l))
```

```python
import functools
import math

import jax
import jax.numpy as jnp
import numpy as np
from jax import lax
from jax.experimental import pallas as pl
from jax.experimental.pallas import tpu as pltpu

F32 = jnp.float32
BF16 = jnp.bfloat16

D_MODEL = 1024
PAGE_SIZE = 128
RET_HEADS = 4
RET_QK_DIM = 128
RET_V_DIM = 256
RET_QK_W = RET_HEADS * RET_QK_DIM
RET_V_W = RET_HEADS * RET_V_DIM
ROPE_BASE = 10000.0
DIFF_HEADS = 8
DIFF_HEAD_DIM = 64
DIFF_V_DIM = 2 * DIFF_HEAD_DIM
LRU_W = D_MODEL
LRU_BLOCKS = 4
LRU_BLOCK_W = LRU_W // LRU_BLOCKS
CONV_W = 4
LRU_C = 8.0
N_BRANCH = 3
D_FF = 2816
IN_W = 10240
EPS = 1e-6

COL_RV, COL_RG, COL_DQ, COL_DK, COL_DV, COL_LX, COL_GZ = 1, 2, 3, 4, 5, 6, 7

LANES = 128
SUBLANES = 8
VMEM_LIMIT = 56 * 1024 * 1024
NEG_BIG = -1e30


def _cparams(sem):
    return pltpu.CompilerParams(dimension_semantics=sem, vmem_limit_bytes=VMEM_LIMIT)


def _resident(shape, index_map):
    return pl.BlockSpec(shape, index_map, pipeline_mode=pl.Buffered(1))


def _in_proj_kernel(x_ref, g_ref, w_ref, z_ref, qkv_ref, h_scr):
    j = pl.program_id(1)

    @pl.when(j == 0)
    def _():
        x = x_ref[...]
        ms = jnp.mean(x * x, axis=-1, keepdims=True)
        h_scr[...] = (x * lax.rsqrt(ms + EPS) * g_ref[...]).astype(BF16)

    z = jnp.dot(h_scr[...], w_ref[...], preferred_element_type=F32)
    z_ref[...] = z

    @pl.when(j == COL_DQ)
    def _():
        qkv_ref[...] = (z * (DIFF_HEAD_DIM ** -0.5)).astype(BF16)

    @pl.when((j == COL_DK) | (j == COL_DV))
    def _():
        qkv_ref[...] = z.astype(BF16)


def in_proj(x, g, w_bf, tm):
    m = x.shape[0]
    tn = 1024
    return pl.pallas_call(
        _in_proj_kernel,
        grid=(m // tm, IN_W // tn),
        in_specs=[
            pl.BlockSpec((tm, D_MODEL), lambda i, j: (i, 0)),
            pl.BlockSpec((1, D_MODEL), lambda i, j: (0, 0)),
            pl.BlockSpec((D_MODEL, tn), lambda i, j: (0, j)),
        ],
        out_specs=[
            pl.BlockSpec((tm, tn), lambda i, j: (i, j)),
            pl.BlockSpec((tm, tn), lambda i, j: (i, jnp.clip(j - COL_DQ, 0, 2))),
        ],
        out_shape=[
            jax.ShapeDtypeStruct((m, IN_W), F32),
            jax.ShapeDtypeStruct((m, 3 * 1024), BF16),
        ],
        scratch_shapes=[pltpu.VMEM((tm, D_MODEL), BF16)],
        compiler_params=_cparams(("arbitrary", "arbitrary")),
        name="in_proj",
    )(x, g.reshape(1, D_MODEL), w_bf)


def _retention_tables(pos, c, cp):
    half = RET_QK_DIM // 2
    inv = ROPE_BASE ** (-jnp.arange(half, dtype=F32) / half)
    ang = pos.astype(F32)[:, None] * inv[None, :]
    cos, sin = jnp.cos(ang), jnp.sin(ang)
    cos2 = jnp.concatenate([cos, cos], axis=-1)
    sin2 = jnp.concatenate([-sin, sin], axis=-1)
    lg = np.log1p(-(2.0 ** (-5.0 - np.arange(RET_HEADS, dtype=np.float64))))
    idx = np.arange(cp, dtype=np.float64)
    rel = idx[:, None] - idx[None, :]
    ok = (rel >= 0) & (idx[:, None] < c) & (idx[None, :] < c)
    dmask = np.where(ok[None], np.exp(lg[:, None, None] * np.maximum(rel, 0.0)[None]), 0.0)
    valid = (idx < c)[:, None]
    rd = np.where(valid, np.exp((idx[:, None] + 1.0) * lg[None, :]), 0.0)
    wd = np.where(valid, np.exp((c - 1.0 - idx)[:, None] * lg[None, :]), 0.0)
    wd = wd * (RET_QK_DIM ** -0.5)
    rd = np.repeat(rd, RET_QK_DIM, axis=1)
    wd = np.repeat(wd, RET_QK_DIM, axis=1)
    cd = np.repeat(np.exp(c * lg), RET_V_DIM)[None, :]
    return (cos2, sin2, jnp.asarray(dmask, F32), jnp.asarray(rd, F32), jnp.asarray(wd, F32),
            jnp.asarray(cd, F32))


def _rope(x, cos2, sin2):
    return x * cos2 + pltpu.roll(x, RET_QK_DIM // 2, axis=1) * sin2


def _retention_kernel(rq_ref, rk_ref, rv_ref, rg_ref, cos_ref, sin_ref, dm_ref, rd_ref, wd_ref,
                      cd_ref, gn_ref, s0_ref, ro_ref, s_out_ref, s_scr, *, c, cp, n_chunks):
    ci = pl.program_id(1)

    @pl.when(ci == 0)
    def _():
        s_scr[...] = s0_ref[0]

    def pad(a):
        if c == cp:
            return a
        return jnp.concatenate([a, jnp.zeros((cp - c, a.shape[1]), a.dtype)], axis=0)

    cos2 = pad(cos_ref[...])
    sin2 = pad(sin_ref[...])
    rq = pad(rq_ref[...])
    rk = pad(rk_ref[...])
    rv = pad(rv_ref[...])
    rd = rd_ref[...]
    wd = wd_ref[...]
    outs = []
    for h in range(RET_HEADS):
        qs = slice(h * RET_QK_DIM, (h + 1) * RET_QK_DIM)
        vs = slice(h * RET_V_DIM, (h + 1) * RET_V_DIM)
        q = _rope(rq[:, qs], cos2, sin2)
        k = _rope(rk[:, qs], cos2, sin2)
        v = rv[:, vs].astype(BF16)
        s_old = s_scr[h]
        sc = lax.dot_general(q.astype(BF16), (k * (RET_QK_DIM ** -0.5)).astype(BF16),
                             (((1,), (1,)), ((), ())), preferred_element_type=F32)
        sc = sc * dm_ref[h]
        o = jnp.dot(sc.astype(BF16), v, preferred_element_type=F32)
        o = o + jnp.dot((q * rd[:, qs]).astype(BF16), s_old.astype(BF16), preferred_element_type=F32)
        kw = (k * wd[:, qs]).T.astype(BF16)
        s_scr[h] = s_old * cd_ref[:, vs] + jnp.dot(kw, v, preferred_element_type=F32)
        mu = jnp.mean(o, axis=-1, keepdims=True)
        d = o - mu
        var = jnp.mean(d * d, axis=-1, keepdims=True)
        outs.append(d * lax.rsqrt(var + EPS))
    o = jnp.concatenate(outs, axis=-1)[:c]
    rg = rg_ref[...]
    ro_ref[...] = (o * gn_ref[...] * (rg * jax.nn.sigmoid(rg))).astype(ro_ref.dtype)

    @pl.when(ci == n_chunks - 1)
    def _():
        s_out_ref[0] = s_scr[...]


def retention(z, pos, s0, gn_g, batch, length, c, cp, out_dtype):
    n_chunks = length // c
    cos2, sin2, dmask, rd, wd, cd = _retention_tables(pos, c, cp)
    kern = functools.partial(_retention_kernel, c=c, cp=cp, n_chunks=n_chunks)
    row = lambda b, i: b * n_chunks + i
    const2 = lambda b, i: (0, 0)
    return pl.pallas_call(
        kern,
        grid=(batch, n_chunks),
        in_specs=[
            pl.BlockSpec((c, RET_QK_W), lambda b, i: (row(b, i), 0)),
            pl.BlockSpec((c, RET_QK_W), lambda b, i: (row(b, i), 1)),
            pl.BlockSpec((c, RET_V_W), lambda b, i: (row(b, i), COL_RV)),
            pl.BlockSpec((c, RET_V_W), lambda b, i: (row(b, i), COL_RG)),
            pl.BlockSpec((c, RET_QK_DIM), lambda b, i: (i, 0)),
            pl.BlockSpec((c, RET_QK_DIM), lambda b, i: (i, 0)),
            pl.BlockSpec((RET_HEADS, cp, cp), lambda b, i: (0, 0, 0)),
            pl.BlockSpec((cp, RET_QK_W), const2),
            pl.BlockSpec((cp, RET_QK_W), const2),
            pl.BlockSpec((1, RET_V_W), const2),
            pl.BlockSpec((1, RET_V_W), const2),
            pl.BlockSpec((1, RET_HEADS, RET_QK_DIM, RET_V_DIM), lambda b, i: (b, 0, 0, 0)),
        ],
        out_specs=[
            pl.BlockSpec((c, RET_V_W), lambda b, i: (row(b, i), 0)),
            pl.BlockSpec((1, RET_HEADS, RET_QK_DIM, RET_V_DIM), lambda b, i: (b, 0, 0, 0)),
        ],
        out_shape=[
            jax.ShapeDtypeStruct((batch * length, RET_V_W), out_dtype),
            jax.ShapeDtypeStruct((batch, RET_HEADS, RET_QK_DIM, RET_V_DIM), F32),
        ],
        scratch_shapes=[pltpu.VMEM((RET_HEADS, RET_QK_DIM, RET_V_DIM), F32)],
        compiler_params=_cparams(("arbitrary", "arbitrary")),
        name="retention",
    )(z, z, z, z, cos2, sin2, dmask, rd, wd, cd, gn_g.reshape(1, RET_V_W), s0)


def _lru_kernel(lx_ref, cbuf_ref, h0_ref, cw_ref, cb_ref, wg_ref, ba_ref, bx_ref, lam_ref,
                lo_ref, cnew_ref, hlast_ref, xe_scr, h_scr, a_scr, b_scr, *, t, n_chunks):
    ci = pl.program_id(1)
    keep = CONV_W - 1
    base = SUBLANES - keep

    @pl.when(ci == 0)
    def _():
        xe_scr[base:SUBLANES, :] = cbuf_ref[0]
        h_scr[...] = h0_ref[0]

    xe_scr[SUBLANES:SUBLANES + t, :] = lx_ref[...]
    xc = cb_ref[...] + xe_scr[base:base + t, :] * cw_ref[0:1, :]
    for j in range(1, CONV_W):
        xc = xc + xe_scr[base + j:base + j + t, :] * cw_ref[j:j + 1, :]
    tail = xe_scr[t + base:t + SUBLANES, :]
    xe_scr[base:SUBLANES, :] = tail

    sp = jax.nn.softplus(-lam_ref[...])
    for n in range(LRU_BLOCKS):
        ws = slice(n * LRU_BLOCK_W, (n + 1) * LRU_BLOCK_W)
        xb = xc[:, ws]
        gates = jnp.dot(xb.astype(BF16), wg_ref[n], preferred_element_type=F32)
        r = jax.nn.sigmoid(gates[:, :LRU_BLOCK_W] + ba_ref[:, ws])
        ig = jax.nn.sigmoid(gates[:, LRU_BLOCK_W:] + bx_ref[:, ws])
        log_a = (-LRU_C) * r * sp[:, ws]
        a = jnp.exp(log_a)
        a_scr[:, ws] = a
        b_scr[:, ws] = jnp.sqrt(1.0 - a * a) * (ig * xb)

    def body(i, h):
        r0 = pl.multiple_of(i * SUBLANES, SUBLANES)
        a8 = a_scr[pl.ds(r0, SUBLANES), :]
        b8 = b_scr[pl.ds(r0, SUBLANES), :]
        rows = []
        for s in range(SUBLANES):
            h = a8[s:s + 1, :] * h + b8[s:s + 1, :]
            rows.append(h)
        lo_ref[pl.ds(r0, SUBLANES), :] = jnp.concatenate(rows, axis=0).astype(lo_ref.dtype)
        return h

    h = lax.fori_loop(0, t // SUBLANES, body, h_scr[...])
    h_scr[...] = h

    @pl.when(ci == n_chunks - 1)
    def _():
        cnew_ref[0] = tail
        hlast_ref[0] = h


def conv_lru(z, cbuf, h0, cw, cb, wg_bf, ba, bx, lam, batch, length, t, out_dtype):
    n_chunks = length // t
    kern = functools.partial(_lru_kernel, t=t, n_chunks=n_chunks)
    c2 = lambda b, i: (0, 0)
    return pl.pallas_call(
        kern,
        grid=(batch, n_chunks),
        in_specs=[
            pl.BlockSpec((t, LRU_W), lambda b, i: (b * n_chunks + i, COL_LX)),
            pl.BlockSpec((1, CONV_W - 1, LRU_W), lambda b, i: (b, 0, 0)),
            pl.BlockSpec((1, 1, LRU_W), lambda b, i: (b, 0, 0)),
            pl.BlockSpec((CONV_W, LRU_W), c2),
            pl.BlockSpec((1, LRU_W), c2),
            pl.BlockSpec((LRU_BLOCKS, LRU_BLOCK_W, 2 * LRU_BLOCK_W), lambda b, i: (0, 0, 0)),
            pl.BlockSpec((1, LRU_W), c2),
            pl.BlockSpec((1, LRU_W), c2),
            pl.BlockSpec((1, LRU_W), c2),
        ],
        out_specs=[
            pl.BlockSpec((t, LRU_W), lambda b, i: (b * n_chunks + i, 0)),
            pl.BlockSpec((1, CONV_W - 1, LRU_W), lambda b, i: (b, 0, 0)),
            pl.BlockSpec((1, 1, LRU_W), lambda b, i: (b, 0, 0)),
        ],
        out_shape=[
            jax.ShapeDtypeStruct((batch * length, LRU_W), out_dtype),
            jax.ShapeDtypeStruct((batch, CONV_W - 1, LRU_W), F32),
            jax.ShapeDtypeStruct((batch, 1, LRU_W), F32),
        ],
        scratch_shapes=[
            pltpu.VMEM((t + SUBLANES, LRU_W), F32),
            pltpu.VMEM((1, LRU_W), F32),
            pltpu.VMEM((t, LRU_W), F32),
            pltpu.VMEM((t, LRU_W), F32),
        ],
        compiler_params=_cparams(("arbitrary", "arbitrary")),
        name="conv_lru",
    )(z, cbuf, h0.reshape(batch, 1, LRU_W), cw, cb.reshape(1, LRU_W), wg_bf,
      ba.reshape(1, LRU_W), bx.reshape(1, LRU_W), lam.reshape(1, LRU_W))


def _diff_out(acc, l, lam, g, post_scale, tq):
    a = acc / l
    o = a[:tq] - lam * a[tq:]
    ms = jnp.mean(o * o, axis=-1, keepdims=True)
    return o * lax.rsqrt(ms + EPS) * g * post_scale


def _attn_prompt_kernel(lam_ref, q_ref, k_ref, v_ref, g_ref, o_ref, *, tq, tk, post_scale):
    i = pl.program_id(1)
    half = DIFF_HEAD_DIM
    q = q_ref[...]
    lane = lax.broadcasted_iota(jnp.int32, (tq, 2 * half), 1)
    zero = jnp.zeros_like(q)
    q2 = jnp.concatenate([jnp.where(lane < half, q, zero), jnp.where(lane >= half, q, zero)], axis=0)

    def scores(j):
        k = k_ref[pl.ds(pl.multiple_of(j * tk, tk), tk), :]
        return lax.dot_general(q2, k, (((1,), (1,)), ((), ())), preferred_element_type=F32)

    def update(j, s, carry):
        m, l, acc = carry
        m_new = jnp.maximum(m, jnp.max(s, axis=-1, keepdims=True))
        alpha = jnp.exp(m - m_new)
        p = jnp.exp(s - m_new)
        l = alpha * l + jnp.sum(p, axis=-1, keepdims=True)
        v = v_ref[pl.ds(pl.multiple_of(j * tk, tk), tk), :]
        acc = alpha * acc + jnp.dot(p.astype(BF16), v, preferred_element_type=F32)
        return m_new, l, acc

    def full_tile(j, carry):
        return update(j, scores(j), carry)

    init = (jnp.full((2 * tq, 1), NEG_BIG, F32), jnp.zeros((2 * tq, 1), F32),
            jnp.zeros((2 * tq, DIFF_V_DIM), F32))
    j_diag = (i * tq) // tk
    carry = lax.fori_loop(0, j_diag, full_tile, init)
    s = scores(j_diag)
    q_pos = i * tq + lax.broadcasted_iota(jnp.int32, (2 * tq, tk), 0) % tq
    k_pos = j_diag * tk + lax.broadcasted_iota(jnp.int32, (2 * tq, tk), 1)
    s = jnp.where(k_pos <= q_pos, s, NEG_BIG)
    _, l, acc = update(j_diag, s, carry)
    o_ref[...] = _diff_out(acc, l, lam_ref[0], g_ref[...], post_scale, tq).astype(o_ref.dtype)


def attn_prompt(qkv, lam, g, seq, tq, tk, post_scale):
    kern = functools.partial(_attn_prompt_kernel, tq=tq, tk=tk, post_scale=post_scale)
    return pl.pallas_call(
        kern,
        grid=(DIFF_HEADS, seq // tq),
        in_specs=[
            pl.BlockSpec(memory_space=pltpu.SMEM),
            pl.BlockSpec((tq, DIFF_V_DIM), lambda h, i: (i, h)),
            pl.BlockSpec((seq, DIFF_V_DIM), lambda h, i: (0, DIFF_HEADS + h)),
            pl.BlockSpec((seq, DIFF_V_DIM), lambda h, i: (0, 2 * DIFF_HEADS + h)),
            pl.BlockSpec((1, DIFF_V_DIM), lambda h, i: (0, 0)),
        ],
        out_specs=pl.BlockSpec((tq, DIFF_V_DIM), lambda h, i: (i, h)),
        out_shape=jax.ShapeDtypeStruct((seq, DIFF_HEADS * DIFF_V_DIM), BF16),
        compiler_params=_cparams(("arbitrary", "arbitrary")),
        name="attn_prompt",
    )(lam.reshape(1), qkv, qkv, qkv, g.reshape(1, DIFF_V_DIM))


def _attn_sample_kernel(pt_ref, lam_ref, wq_ref, kn_ref, vn_ref, g_ref, *rest,
                        pages, n_steps, n_new, post_scale):
    k_refs = rest[:pages]
    v_refs = rest[pages:2 * pages]
    o_ref = rest[2 * pages]
    m_scr, l_scr, acc_scr = rest[2 * pages + 1:]
    step = pl.program_id(1)
    rows = 2 * DIFF_HEADS * n_new
    cols = PAGE_SIZE * DIFF_HEADS

    @pl.when(step == 0)
    def _():
        m_scr[...] = jnp.full(m_scr.shape, NEG_BIG, F32)
        l_scr[...] = jnp.zeros(l_scr.shape, F32)
        acc_scr[...] = jnp.zeros(acc_scr.shape, F32)

    wq = wq_ref[0]
    row_head = lax.broadcasted_iota(jnp.int32, (rows, cols), 0) // (2 * n_new)
    col_head = lax.broadcasted_iota(jnp.int32, (rows, cols), 1) % DIFF_HEADS
    same_head = row_head == col_head

    def online(s_list, v_list):
        m_old = m_scr[...]
        m_new = m_old
        for s in s_list:
            m_new = jnp.maximum(m_new, jnp.max(s, axis=-1, keepdims=True))
        alpha = jnp.exp(m_old - m_new)
        l = alpha * l_scr[...]
        acc = alpha * acc_scr[...]
        for s, v in zip(s_list, v_list):
            p = jnp.exp(s - m_new)
            l = l + jnp.sum(p, axis=-1, keepdims=True)
            acc = acc + jnp.dot(p.astype(BF16), v, preferred_element_type=F32)
        m_scr[...] = m_new
        l_scr[...] = l
        acc_scr[...] = acc

    s_list, v_list = [], []
    for g in range(pages):
        k2 = k_refs[g][...].reshape(cols, DIFF_V_DIM).astype(BF16)
        v2 = v_refs[g][...].reshape(cols, DIFF_V_DIM).astype(BF16)
        s = lax.dot_general(wq, k2, (((1,), (1,)), ((), ())), preferred_element_type=F32)
        s_list.append(jnp.where(same_head, s, NEG_BIG))
        v_list.append(v2)
    online(s_list, v_list)

    @pl.when(step == n_steps - 1)
    def _():
        kn = kn_ref[0]
        vn = vn_ref[0]
        ncol = kn.shape[0]
        s = lax.dot_general(wq, kn, (((1,), (1,)), ((), ())), preferred_element_type=F32)
        r = lax.broadcasted_iota(jnp.int32, (rows, ncol), 0)
        c = lax.broadcasted_iota(jnp.int32, (rows, ncol), 1)
        ok = ((r // (2 * n_new)) == (c % DIFF_HEADS)) & ((c // DIFF_HEADS) <= (r % n_new)) \
            & (c < n_new * DIFF_HEADS)
        online([jnp.where(ok, s, NEG_BIG)], [vn])
        a = acc_scr[...] / l_scr[...]
        lam = lam_ref[0]
        outs = []
        for h in range(DIFF_HEADS):
            r0 = h * 2 * n_new
            o = a[r0:r0 + n_new] - lam * a[r0 + n_new:r0 + 2 * n_new]
            ms = jnp.mean(o * o, axis=-1, keepdims=True)
            outs.append(o * lax.rsqrt(ms + EPS) * g_ref[...] * post_scale)
        o_ref[0] = jnp.concatenate(outs, axis=-1)


def attn_sample(wq, k_new, v_new, lam, g, cache_k, cache_v, page_table, layer, pages, post_scale):
    batch, n_pages = page_table.shape
    n_new = wq.shape[1] // (2 * DIFF_HEADS)
    n_steps = n_pages // pages
    kern = functools.partial(_attn_sample_kernel, pages=pages, n_steps=n_steps, n_new=n_new,
                             post_scale=post_scale)

    def page_spec(gi):
        return pl.BlockSpec((None, None, PAGE_SIZE, DIFF_HEADS, DIFF_V_DIM),
                            lambda b, p, pt: (layer, pt[b, p * pages + gi], 0, 0, 0))

    per_b = lambda b, p, pt: (b, 0, 0)
    rows = wq.shape[1]
    grid_spec = pltpu.PrefetchScalarGridSpec(
        num_scalar_prefetch=1,
        grid=(batch, n_steps),
        in_specs=[
            pl.BlockSpec(memory_space=pltpu.SMEM),
            pl.BlockSpec((1, rows, DIFF_V_DIM), per_b),
            pl.BlockSpec((1,) + k_new.shape[1:], per_b),
            pl.BlockSpec((1,) + v_new.shape[1:], per_b),
            pl.BlockSpec((1, DIFF_V_DIM), lambda b, p, pt: (0, 0)),
        ] + [page_spec(gi) for gi in range(pages)] * 2,
        out_specs=pl.BlockSpec((1, n_new, DIFF_HEADS * DIFF_V_DIM), per_b),
        scratch_shapes=[
            pltpu.VMEM((rows, 1), F32),
            pltpu.VMEM((rows, 1), F32),
            pltpu.VMEM((rows, DIFF_V_DIM), F32),
        ],
    )
    return pl.pallas_call(
        kern,
        grid_spec=grid_spec,
        out_shape=jax.ShapeDtypeStruct((batch, n_new, DIFF_HEADS * DIFF_V_DIM), F32),
        compiler_params=_cparams(("arbitrary", "arbitrary")),
        name="attn_sample",
    )(page_table, lam.reshape(1), wq, k_new, v_new, g.reshape(1, DIFF_V_DIM),
      *([cache_k] * pages), *([cache_v] * pages))


def _merge_kernel(x_ref, ro_ref, do_ref, lo_ref, g0_ref, g1_ref, g2_ref, wb_ref, wo_ref, y_ref):
    mix = None
    for n, (br, gz) in enumerate(((ro_ref, g0_ref), (do_ref, g1_ref), (lo_ref, g2_ref))):
        proj = jnp.dot(br[...].astype(BF16), wb_ref[n], preferred_element_type=F32)
        term = jax.nn.sigmoid(gz[...]) * proj
        mix = term if mix is None else mix + term
    y_ref[...] = x_ref[...] + jnp.dot(mix.astype(BF16), wo_ref[...], preferred_element_type=F32)


def merge(x, ro, do, lo, z, wb_bf, wo_bf, tm):
    m = x.shape[0]
    row = lambda i: (i, 0)
    return pl.pallas_call(
        _merge_kernel,
        grid=(m // tm,),
        in_specs=[
            pl.BlockSpec((tm, D_MODEL), row),
            pl.BlockSpec((tm, D_MODEL), row),
            pl.BlockSpec((tm, D_MODEL), row),
            pl.BlockSpec((tm, D_MODEL), row),
            pl.BlockSpec((tm, D_MODEL), lambda i: (i, COL_GZ)),
            pl.BlockSpec((tm, D_MODEL), lambda i: (i, COL_GZ + 1)),
            pl.BlockSpec((tm, D_MODEL), lambda i: (i, COL_GZ + 2)),
            _resident((N_BRANCH, D_MODEL, D_MODEL), lambda i: (0, 0, 0)),
            _resident((D_MODEL, D_MODEL), lambda i: (0, 0)),
        ],
        out_specs=pl.BlockSpec((tm, D_MODEL), row),
        out_shape=jax.ShapeDtypeStruct((m, D_MODEL), F32),
        compiler_params=_cparams(("arbitrary",)),
        name="merge",
    )(x, ro, do, lo, z, z, z, wb_bf, wo_bf)


FF_CHUNK = D_FF // 2


def _ffn_kernel(x_ref, g_ref, wg_ref, wu_ref, wd_ref, fg_ref, y_ref, *, final_norm):
    x = x_ref[...]
    ms = jnp.mean(x * x, axis=-1, keepdims=True)
    h = (x * lax.rsqrt(ms + EPS) * g_ref[...]).astype(BF16)
    y = x
    for c in range(D_FF // FF_CHUNK):
        cs = slice(c * FF_CHUNK, (c + 1) * FF_CHUNK)
        gate = jnp.dot(h, wg_ref[:, cs], preferred_element_type=F32)
        up = jnp.dot(h, wu_ref[:, cs], preferred_element_type=F32)
        act = (gate * jax.nn.sigmoid(gate) * up).astype(BF16)
        y = y + jnp.dot(act, wd_ref[cs, :], preferred_element_type=F32)
    if final_norm:
        ms = jnp.mean(y * y, axis=-1, keepdims=True)
        y = y * lax.rsqrt(ms + EPS) * fg_ref[...]
    y_ref[...] = y


def ffn(x, g, wg_bf, wu_bf, wd_bf, final_g, tm, final_norm):
    m = x.shape[0]
    c2 = lambda i: (0, 0)
    return pl.pallas_call(
        functools.partial(_ffn_kernel, final_norm=final_norm),
        grid=(m // tm,),
        in_specs=[
            pl.BlockSpec((tm, D_MODEL), lambda i: (i, 0)),
            pl.BlockSpec((1, D_MODEL), c2),
            _resident((D_MODEL, D_FF), c2),
            _resident((D_MODEL, D_FF), c2),
            _resident((D_FF, D_MODEL), c2),
            pl.BlockSpec((1, D_MODEL), c2),
        ],
        out_specs=pl.BlockSpec((tm, D_MODEL), lambda i: (i, 0)),
        out_shape=jax.ShapeDtypeStruct((m, D_MODEL), F32),
        compiler_params=_cparams(("arbitrary",)),
        name="ffn",
    )(x, g.reshape(1, D_MODEL), wg_bf, wu_bf, wd_bf, final_g.reshape(1, D_MODEL))


def _sample_query_blocks(dq, n_new):
    b = dq.shape[0] // n_new
    q = dq.reshape(b, n_new, DIFF_HEADS, 2, DIFF_HEAD_DIM).transpose(0, 2, 3, 1, 4)
    eye = jnp.eye(2, dtype=q.dtype)
    blk = q[:, :, :, :, None, :] * eye[None, None, :, None, :, None]
    return blk.reshape(b, DIFF_HEADS * 2 * n_new, 2 * DIFF_HEAD_DIM)


def _new_token_rows(x, n_new):
    b = x.shape[0] // n_new
    r = x.reshape(b, n_new * DIFF_HEADS, DIFF_V_DIM)
    return jnp.pad(r, ((0, 0), (0, LANES - n_new * DIFF_HEADS), (0, 0)))


def kernel(x_prompt, x_sample, cache_k, cache_v, page_table, state_ret, state_conv, state_lru,
           attn_norm_g, w_in, ret_gn_g, diff_lambda, diff_ln_g, conv_w, conv_b,
           gate_a_w, gate_a_b, gate_x_w, gate_x_b, lru_lambda, w_branch, w_out,
           ffn_norm_g, w_gate, w_up, w_down, final_norm_g):
    bp, seq, _ = x_prompt.shape
    bs, n_new, _ = x_sample.shape
    depth = w_in.shape[0]
    past = page_table.shape[1] * PAGE_SIZE
    pos_p = jnp.arange(seq)
    pos_s = past + jnp.arange(n_new)

    xp = x_prompt.reshape(bp * seq, D_MODEL)
    xs = x_sample.reshape(bs * n_new, D_MODEL)
    ret0 = jnp.zeros((bp, RET_HEADS, RET_QK_DIM, RET_V_DIM), F32)
    conv0 = jnp.zeros((bp, CONV_W - 1, LRU_W), F32)
    h00 = jnp.zeros((bp, LRU_W), F32)

    tm_p = 512
    tm_s = bs * n_new
    outs = {k: [] for k in ("kp", "vp", "rp", "cp", "hp", "ks", "vs", "rs", "cs", "hs")}
    for l in range(depth):
        lam_init = 0.8 - 0.6 * math.exp(-0.3 * l)
        post = 1.0 - lam_init
        lq1, lk1, lq2, lk2 = diff_lambda[l].astype(F32)
        lam = jnp.exp(jnp.sum(lq1 * lk1)) - jnp.exp(jnp.sum(lq2 * lk2)) + lam_init
        w_in_bf = w_in[l].astype(BF16)
        wb_bf = w_branch[l].astype(BF16)
        wo_bf = w_out[l].astype(BF16)
        wg_bf = w_gate[l].astype(BF16)
        wu_bf = w_up[l].astype(BF16)
        wd_bf = w_down[l].astype(BF16)
        wgate_bf = jnp.concatenate([gate_a_w[l], gate_x_w[l]], axis=-1).astype(BF16)
        last = l == depth - 1

        z, qkv = in_proj(xp, attn_norm_g[l], w_in_bf, tm_p)
        ro, r1 = retention(z, pos_p, ret0, ret_gn_g[l], bp, seq, 256, 256, BF16)
        do = attn_prompt(qkv, lam, diff_ln_g[l], seq, 256, 512, post)
        lo, c1, h1 = conv_lru(z, conv0, h00, conv_w[l], conv_b[l], wgate_bf, gate_a_b[l],
                              gate_x_b[l], lru_lambda[l], bp, seq, 256, BF16)
        xp = merge(xp, ro, do, lo, z, wb_bf, wo_bf, tm_p)
        xp = ffn(xp, ffn_norm_g[l], wg_bf, wu_bf, wd_bf, final_norm_g, tm_p, last)
        outs["kp"].append(z[:, COL_DK * 1024:(COL_DK + 1) * 1024].reshape(bp, seq, DIFF_HEADS, DIFF_V_DIM))
        outs["vp"].append(z[:, COL_DV * 1024:(COL_DV + 1) * 1024].reshape(bp, seq, DIFF_HEADS, DIFF_V_DIM))
        outs["rp"].append(r1)
        outs["cp"].append(c1)
        outs["hp"].append(h1.reshape(bp, LRU_W))

        z, qkv = in_proj(xs, attn_norm_g[l], w_in_bf, tm_s)
        ro, r2 = retention(z, pos_s, state_ret[l], ret_gn_g[l], bs, n_new, n_new, LANES, F32)
        wq = _sample_query_blocks(qkv[:, :1024], n_new)
        k_new = _new_token_rows(qkv[:, 1024:2048], n_new)
        v_new = _new_token_rows(qkv[:, 2048:], n_new)
        do = attn_sample(wq, k_new, v_new, lam, diff_ln_g[l], cache_k, cache_v, page_table, l, 4, post)
        do = do.reshape(bs * n_new, DIFF_HEADS * DIFF_V_DIM)
        lo, c2, h2 = conv_lru(z, state_conv[l], state_lru[l], conv_w[l], conv_b[l], wgate_bf,
                              gate_a_b[l], gate_x_b[l], lru_lambda[l], bs, n_new, n_new, F32)
        xs = merge(xs, ro, do, lo, z, wb_bf, wo_bf, tm_s)
        xs = ffn(xs, ffn_norm_g[l], wg_bf, wu_bf, wd_bf, final_norm_g, tm_s, last)
        outs["ks"].append(z[:, COL_DK * 1024:(COL_DK + 1) * 1024].reshape(bs, n_new, DIFF_HEADS, DIFF_V_DIM))
        outs["vs"].append(z[:, COL_DV * 1024:(COL_DV + 1) * 1024].reshape(bs, n_new, DIFF_HEADS, DIFF_V_DIM))
        outs["rs"].append(r2)
        outs["cs"].append(c2)
        outs["hs"].append(h2.reshape(bs, LRU_W))

    st = lambda k: jnp.stack(outs[k])
    return (xp.reshape(bp, seq, D_MODEL), xs.reshape(bs, n_new, D_MODEL),
            st("kp"), st("vp"), st("rp"), st("cp"), st("hp"),
            st("ks"), st("vs"), st("rs"), st("cs"), st("hs"))
```

```python
import functools
import math

import jax
import jax.numpy as jnp
import numpy as np
from jax import lax
from jax.experimental import pallas as pl
from jax.experimental.pallas import tpu as pltpu

F32 = jnp.float32
BF16 = jnp.bfloat16

D_MODEL = 1024
PAGE_SIZE = 128
RET_HEADS = 4
RET_QK_DIM = 128
RET_V_DIM = 256
RET_QK_W = RET_HEADS * RET_QK_DIM
RET_V_W = RET_HEADS * RET_V_DIM
ROPE_BASE = 10000.0
DIFF_HEADS = 8
DIFF_HEAD_DIM = 64
DIFF_V_DIM = 2 * DIFF_HEAD_DIM
DIFF_W = DIFF_HEADS * DIFF_V_DIM
LRU_W = D_MODEL
LRU_BLOCKS = 4
LRU_BLOCK_W = LRU_W // LRU_BLOCKS
CONV_W = 4
LRU_C = 8.0
N_BRANCH = 3
D_FF = 2816
IN_W = 10240
EPS = 1e-6
LOG2E = 1.4426950408889634

COL_DQ, COL_DK, COL_DV, COL_LX, COL_GZ = 3, 4, 5, 6, 7

LANES = 128
SUBLANES = 8
BF16_ROWS = 16
VMEM_LIMIT = 56 * 1024 * 1024
NEG_BIG = -1e30

TM_PROJ = 256
TM_DENSE = 512
RET_CHUNK = 256
LRU_CHUNK = 256
ATT_TQ = 512
ATT_TK = 512
PAGES_PER_STEP = 8


def _cparams(sem):
    return pltpu.CompilerParams(dimension_semantics=sem, vmem_limit_bytes=VMEM_LIMIT)


def _resident(shape, index_map):
    return pl.BlockSpec(shape, index_map, pipeline_mode=pl.Buffered(1))


def _in_proj_kernel(x_ref, g_ref, w_ref, zr_ref, q_ref, k_ref, v_ref, kf_ref, vf_ref, lx_ref, gz_ref,
                    *, v_transposed):
    x = x_ref[...]
    ms = jnp.mean(x * x, axis=-1, keepdims=True)
    h = (x * lax.rsqrt(ms + EPS) * g_ref[...]).astype(BF16)

    def group(j):
        return jnp.dot(h, w_ref[:, j * 1024:(j + 1) * 1024], preferred_element_type=F32)

    for j in range(3):
        zr_ref[:, j * 1024:(j + 1) * 1024] = group(j).astype(zr_ref.dtype)
    q_ref[...] = (group(COL_DQ) * (LOG2E * DIFF_HEAD_DIM ** -0.5)).astype(BF16)
    zk = group(COL_DK)
    kf_ref[...] = zk
    k_ref[...] = zk.astype(BF16)
    zv = group(COL_DV)
    vf_ref[...] = zv
    if v_transposed:
        v_ref[0] = zv.T.astype(BF16)
    else:
        v_ref[...] = zv.astype(BF16)
    lx_ref[...] = group(COL_LX)
    for j in range(3):
        gz_ref[:, j * 1024:(j + 1) * 1024] = group(COL_GZ + j).astype(BF16)


def in_proj(x, g, w_bf, tm, v_transposed, zr_dtype):
    m = x.shape[0]
    row = lambda i: (i, 0)
    wide = lambda w, dt: (pl.BlockSpec((tm, w), row), jax.ShapeDtypeStruct((m, w), dt))
    if v_transposed:
        v_out = (pl.BlockSpec((1, DIFF_W, tm), lambda i: (i, 0, 0)),
                 jax.ShapeDtypeStruct((m // tm, DIFF_W, tm), BF16))
    else:
        v_out = wide(DIFF_W, BF16)
    outs = [wide(3 * 1024, zr_dtype), wide(DIFF_W, BF16), wide(DIFF_W, BF16), v_out,
            wide(DIFF_W, F32), wide(DIFF_W, F32), wide(LRU_W, F32), wide(3 * 1024, BF16)]
    return pl.pallas_call(
        functools.partial(_in_proj_kernel, v_transposed=v_transposed),
        grid=(m // tm,),
        in_specs=[
            pl.BlockSpec((tm, D_MODEL), row),
            pl.BlockSpec((1, D_MODEL), lambda i: (0, 0)),
            _resident((D_MODEL, IN_W), lambda i: (0, 0)),
        ],
        out_specs=[o[0] for o in outs],
        out_shape=[o[1] for o in outs],
        compiler_params=_cparams(("arbitrary",)),
        name="in_proj",
    )(x, g.reshape(1, D_MODEL), w_bf)


def _retention_tables(pos, c, cp):
    half = RET_QK_DIM // 2
    inv = ROPE_BASE ** (-jnp.arange(half, dtype=F32) / half)
    ang = pos.astype(F32)[:, None] * inv[None, :]
    cos, sin = jnp.cos(ang), jnp.sin(ang)
    cos2 = jnp.concatenate([cos, cos], axis=-1)
    sin2 = jnp.concatenate([-sin, sin], axis=-1)
    lg = np.log1p(-(2.0 ** (-5.0 - np.arange(RET_HEADS, dtype=np.float64))))
    idx = np.arange(cp, dtype=np.float64)
    rel = idx[:, None] - idx[None, :]
    ok = (rel >= 0) & (idx[:, None] < c) & (idx[None, :] < c)
    dmask = np.where(ok[None], np.exp(lg[:, None, None] * np.maximum(rel, 0.0)[None]), 0.0)
    valid = (idx < c)[:, None]
    rd = np.where(valid, np.exp((idx[:, None] + 1.0) * lg[None, :]), 0.0)
    wd = np.where(valid, np.exp((c - 1.0 - idx)[:, None] * lg[None, :]), 0.0)
    wd = wd * (RET_QK_DIM ** -0.5)
    rd = np.repeat(rd, RET_QK_DIM, axis=1)
    wd = np.repeat(wd, RET_QK_DIM, axis=1)
    cd = np.repeat(np.exp(c * lg), RET_V_DIM)[None, :]
    return (cos2, sin2, jnp.asarray(dmask, F32), jnp.asarray(rd, F32), jnp.asarray(wd, F32),
            jnp.asarray(cd, F32))


def _rope(x, cos2, sin2):
    return x * cos2 + pltpu.roll(x, RET_QK_DIM // 2, axis=1) * sin2


def _retention_kernel(rq_ref, rk_ref, rv_ref, rg_ref, cos_ref, sin_ref, dm_ref, rd_ref, wd_ref,
                      cd_ref, gn_ref, s0_ref, ro_ref, s_out_ref, s_scr, *, c, cp, n_chunks):
    ci = pl.program_id(1)

    @pl.when(ci == 0)
    def _():
        s_scr[...] = s0_ref[0]

    def pad(a):
        if c == cp:
            return a
        return jnp.concatenate([a, jnp.zeros((cp - c, a.shape[1]), a.dtype)], axis=0)

    cos2 = pad(cos_ref[...])
    sin2 = pad(sin_ref[...])
    rq = pad(rq_ref[...].astype(F32))
    rk = pad(rk_ref[...].astype(F32))
    rv = pad(rv_ref[...].astype(F32))
    rd = rd_ref[...]
    wd = wd_ref[...]
    outs = []
    for h in range(RET_HEADS):
        qs = slice(h * RET_QK_DIM, (h + 1) * RET_QK_DIM)
        vs = slice(h * RET_V_DIM, (h + 1) * RET_V_DIM)
        q = _rope(rq[:, qs], cos2, sin2)
        k = _rope(rk[:, qs], cos2, sin2)
        v = rv[:, vs].astype(BF16)
        s_old = s_scr[h]
        sc = lax.dot_general(q.astype(BF16), (k * (RET_QK_DIM ** -0.5)).astype(BF16),
                             (((1,), (1,)), ((), ())), preferred_element_type=F32)
        sc = sc * dm_ref[h]
        o = jnp.dot(sc.astype(BF16), v, preferred_element_type=F32)
        o = o + jnp.dot((q * rd[:, qs]).astype(BF16), s_old.astype(BF16), preferred_element_type=F32)
        kw = (k * wd[:, qs]).T.astype(BF16)
        s_scr[h] = s_old * cd_ref[:, vs] + jnp.dot(kw, v, preferred_element_type=F32)
        mu = jnp.mean(o, axis=-1, keepdims=True)
        d = o - mu
        var = jnp.mean(d * d, axis=-1, keepdims=True)
        outs.append(d * lax.rsqrt(var + EPS))
    o = jnp.concatenate(outs, axis=-1)[:c]
    rg = rg_ref[...].astype(F32)
    ro_ref[...] = (o * gn_ref[...] * (rg * jax.nn.sigmoid(rg))).astype(ro_ref.dtype)

    @pl.when(ci == n_chunks - 1)
    def _():
        s_out_ref[0] = s_scr[...]


def retention(zr, pos, s0, gn_g, batch, length, c, cp, out_dtype):
    n_chunks = length // c
    cos2, sin2, dmask, rd, wd, cd = _retention_tables(pos, c, cp)
    kern = functools.partial(_retention_kernel, c=c, cp=cp, n_chunks=n_chunks)
    row = lambda b, i: b * n_chunks + i
    const2 = lambda b, i: (0, 0)
    return pl.pallas_call(
        kern,
        grid=(batch, n_chunks),
        in_specs=[
            pl.BlockSpec((c, RET_QK_W), lambda b, i: (row(b, i), 0)),
            pl.BlockSpec((c, RET_QK_W), lambda b, i: (row(b, i), 1)),
            pl.BlockSpec((c, RET_V_W), lambda b, i: (row(b, i), 1)),
            pl.BlockSpec((c, RET_V_W), lambda b, i: (row(b, i), 2)),
            pl.BlockSpec((c, RET_QK_DIM), lambda b, i: (i, 0)),
            pl.BlockSpec((c, RET_QK_DIM), lambda b, i: (i, 0)),
            pl.BlockSpec((RET_HEADS, cp, cp), lambda b, i: (0, 0, 0)),
            pl.BlockSpec((cp, RET_QK_W), const2),
            pl.BlockSpec((cp, RET_QK_W), const2),
            pl.BlockSpec((1, RET_V_W), const2),
            pl.BlockSpec((1, RET_V_W), const2),
            pl.BlockSpec((1, RET_HEADS, RET_QK_DIM, RET_V_DIM), lambda b, i: (b, 0, 0, 0)),
        ],
        out_specs=[
            pl.BlockSpec((c, RET_V_W), lambda b, i: (row(b, i), 0)),
            pl.BlockSpec((1, RET_HEADS, RET_QK_DIM, RET_V_DIM), lambda b, i: (b, 0, 0, 0)),
        ],
        out_shape=[
            jax.ShapeDtypeStruct((batch * length, RET_V_W), out_dtype),
            jax.ShapeDtypeStruct((batch, RET_HEADS, RET_QK_DIM, RET_V_DIM), F32),
        ],
        scratch_shapes=[pltpu.VMEM((RET_HEADS, RET_QK_DIM, RET_V_DIM), F32)],
        compiler_params=_cparams(("arbitrary", "arbitrary")),
        name="retention",
    )(zr, zr, zr, zr, cos2, sin2, dmask, rd, wd, cd, gn_g.reshape(1, RET_V_W), s0)


def _lru_kernel(lx_ref, cbuf_ref, h0_ref, cw_ref, cb_ref, wg_ref, ba_ref, bx_ref, lam_ref,
                lo_ref, cnew_ref, hlast_ref, xe_scr, h_scr, a_scr, b_scr, *, t, n_chunks):
    ci = pl.program_id(1)
    keep = CONV_W - 1
    base = SUBLANES - keep

    @pl.when(ci == 0)
    def _():
        xe_scr[base:SUBLANES, :] = cbuf_ref[0]
        h_scr[...] = h0_ref[0]

    xe_scr[SUBLANES:SUBLANES + t, :] = lx_ref[...]
    xc = cb_ref[...] + xe_scr[base:base + t, :] * cw_ref[0:1, :]
    for j in range(1, CONV_W):
        xc = xc + xe_scr[base + j:base + j + t, :] * cw_ref[j:j + 1, :]
    tail = xe_scr[t + base:t + SUBLANES, :]
    xe_scr[base:SUBLANES, :] = tail

    sp = jax.nn.softplus(-lam_ref[...])
    for n in range(LRU_BLOCKS):
        ws = slice(n * LRU_BLOCK_W, (n + 1) * LRU_BLOCK_W)
        xb = xc[:, ws]
        gates = jnp.dot(xb.astype(BF16), wg_ref[n], preferred_element_type=F32)
        r = jax.nn.sigmoid(gates[:, :LRU_BLOCK_W] + ba_ref[:, ws])
        ig = jax.nn.sigmoid(gates[:, LRU_BLOCK_W:] + bx_ref[:, ws])
        log_a = (-LRU_C) * r * sp[:, ws]
        a = jnp.exp(log_a)
        a_scr[:, ws] = a
        b_scr[:, ws] = jnp.sqrt(1.0 - a * a) * (ig * xb)

    def body(i, h):
        r0 = pl.multiple_of(i * SUBLANES, SUBLANES)
        a8 = a_scr[pl.ds(r0, SUBLANES), :]
        b8 = b_scr[pl.ds(r0, SUBLANES), :]
        rows = []
        for s in range(SUBLANES):
            h = a8[s:s + 1, :] * h + b8[s:s + 1, :]
            rows.append(h)
        lo_ref[pl.ds(r0, SUBLANES), :] = jnp.concatenate(rows, axis=0).astype(lo_ref.dtype)
        return h

    h = lax.fori_loop(0, t // SUBLANES, body, h_scr[...])
    h_scr[...] = h

    @pl.when(ci == n_chunks - 1)
    def _():
        cnew_ref[0] = tail
        hlast_ref[0] = h


def conv_lru(lx, cbuf, h0, cw, cb, wg_bf, ba, bx, lam, batch, length, t, out_dtype):
    n_chunks = length // t
    kern = functools.partial(_lru_kernel, t=t, n_chunks=n_chunks)
    c2 = lambda b, i: (0, 0)
    return pl.pallas_call(
        kern,
        grid=(batch, n_chunks),
        in_specs=[
            pl.BlockSpec((t, LRU_W), lambda b, i: (b * n_chunks + i, 0)),
            pl.BlockSpec((1, CONV_W - 1, LRU_W), lambda b, i: (b, 0, 0)),
            pl.BlockSpec((1, 1, LRU_W), lambda b, i: (b, 0, 0)),
            pl.BlockSpec((CONV_W, LRU_W), c2),
            pl.BlockSpec((1, LRU_W), c2),
            pl.BlockSpec((LRU_BLOCKS, LRU_BLOCK_W, 2 * LRU_BLOCK_W), lambda b, i: (0, 0, 0)),
            pl.BlockSpec((1, LRU_W), c2),
            pl.BlockSpec((1, LRU_W), c2),
            pl.BlockSpec((1, LRU_W), c2),
        ],
        out_specs=[
            pl.BlockSpec((t, LRU_W), lambda b, i: (b * n_chunks + i, 0)),
            pl.BlockSpec((1, CONV_W - 1, LRU_W), lambda b, i: (b, 0, 0)),
            pl.BlockSpec((1, 1, LRU_W), lambda b, i: (b, 0, 0)),
        ],
        out_shape=[
            jax.ShapeDtypeStruct((batch * length, LRU_W), out_dtype),
            jax.ShapeDtypeStruct((batch, CONV_W - 1, LRU_W), F32),
            jax.ShapeDtypeStruct((batch, 1, LRU_W), F32),
        ],
        scratch_shapes=[
            pltpu.VMEM((t + SUBLANES, LRU_W), F32),
            pltpu.VMEM((1, LRU_W), F32),
            pltpu.VMEM((t, LRU_W), F32),
            pltpu.VMEM((t, LRU_W), F32),
        ],
        compiler_params=_cparams(("arbitrary", "arbitrary")),
        name="conv_lru",
    )(lx, cbuf, h0.reshape(batch, 1, LRU_W), cw, cb.reshape(1, LRU_W), wg_bf,
      ba.reshape(1, LRU_W), bx.reshape(1, LRU_W), lam.reshape(1, LRU_W))


def _attn_prompt_kernel(lam_ref, q_ref, k_ref, vt_ref, g_ref, bias_ref, o_ref, *, tq, tk, vt_tile,
                        post_scale):
    i = pl.program_id(1)
    half = DIFF_HEAD_DIM
    q = q_ref[...]
    lane = lax.broadcasted_iota(jnp.int32, (tq, 2 * half), 1)
    zero = jnp.zeros_like(q)
    q2 = jnp.concatenate([jnp.where(lane < half, q, zero), jnp.where(lane >= half, q, zero)], axis=0)
    ones = jnp.ones((BF16_ROWS, tk), BF16)
    per = tk // vt_tile

    def update(j, carry, bias):
        m, acc = carry
        k = k_ref[pl.ds(pl.multiple_of(j * tk, tk), tk), :]
        st = lax.dot_general(k, q2, (((1,), (1,)), ((), ())), preferred_element_type=F32)
        if bias is not None:
            st = st + bias
        m_new = jnp.maximum(m, jnp.max(st, axis=0, keepdims=True))
        alpha = jnp.exp2(m - m_new)
        p = jnp.exp2(st - m_new).astype(BF16)
        va = jnp.concatenate([vt_ref[j * per + r] for r in range(per)] , axis=1)
        va = jnp.concatenate([va, ones], axis=0)
        acc = alpha * acc + jnp.dot(va, p, preferred_element_type=F32)
        return m_new, acc

    init = (jnp.full((1, 2 * tq), NEG_BIG, F32), jnp.zeros((DIFF_V_DIM + BF16_ROWS, 2 * tq), F32))
    j_diag = (i * tq) // tk
    carry = lax.fori_loop(0, j_diag, lambda j, c: update(j, c, None), init)
    _, acc = update(j_diag, carry, bias_ref[...])
    a = acc[:DIFF_V_DIM] / acc[DIFF_V_DIM:DIFF_V_DIM + 1]
    ot = a[:, :tq] - lam_ref[0] * a[:, tq:]
    ms = jnp.mean(ot * ot, axis=0, keepdims=True)
    ot = ot * lax.rsqrt(ms + EPS)
    o_ref[...] = (ot.T * g_ref[...] * post_scale).astype(o_ref.dtype)


def attn_prompt(q, k, vt, lam, g, seq, tq, tk, post_scale):
    assert tq == tk, "the diagonal tile is square"
    vt_tile = vt.shape[2]
    kern = functools.partial(_attn_prompt_kernel, tq=tq, tk=tk, vt_tile=vt_tile, post_scale=post_scale)
    kk = np.arange(tk)[:, None]
    qq = np.arange(tq)[None, :]
    bias = np.where(kk <= qq, 0.0, NEG_BIG).astype(np.float32)
    bias = jnp.asarray(np.concatenate([bias, bias], axis=1))
    return pl.pallas_call(
        kern,
        grid=(DIFF_HEADS, seq // tq),
        in_specs=[
            pl.BlockSpec(memory_space=pltpu.SMEM),
            pl.BlockSpec((tq, DIFF_V_DIM), lambda h, i: (i, h)),
            pl.BlockSpec((seq, DIFF_V_DIM), lambda h, i: (0, h)),
            pl.BlockSpec((seq // vt_tile, DIFF_V_DIM, vt_tile), lambda h, i: (0, h, 0)),
            pl.BlockSpec((1, DIFF_V_DIM), lambda h, i: (0, 0)),
            pl.BlockSpec((tk, 2 * tq), lambda h, i: (0, 0)),
        ],
        out_specs=pl.BlockSpec((tq, DIFF_V_DIM), lambda h, i: (i, h)),
        out_shape=jax.ShapeDtypeStruct((seq, DIFF_W), BF16),
        compiler_params=_cparams(("arbitrary", "arbitrary")),
        name="attn_prompt",
    )(lam.reshape(1), q, k, vt, g.reshape(1, DIFF_V_DIM), bias)


def _attn_sample_kernel(pt_ref, lam_ref, wq_ref, kn_ref, vn_ref, g_ref, bias_ref, nbias_ref, *rest,
                        pages, n_steps, n_new, post_scale):
    k_refs = rest[:pages]
    v_refs = rest[pages:2 * pages]
    o_ref = rest[2 * pages]
    m_scr, l_scr, acc_scr = rest[2 * pages + 1:]
    step = pl.program_id(1)
    cols = PAGE_SIZE * DIFF_HEADS

    @pl.when(step == 0)
    def _():
        m_scr[...] = jnp.full(m_scr.shape, NEG_BIG, F32)
        l_scr[...] = jnp.zeros(l_scr.shape, F32)
        acc_scr[...] = jnp.zeros(acc_scr.shape, F32)

    wq = wq_ref[0]

    def online(s_list, v_list):
        m_old = m_scr[...]
        m_new = m_old
        for s in s_list:
            m_new = jnp.maximum(m_new, jnp.max(s, axis=-1, keepdims=True))
        alpha = jnp.exp2(m_old - m_new)
        l = alpha * l_scr[...]
        acc = alpha * acc_scr[...]
        for s, v in zip(s_list, v_list):
            p = jnp.exp2(s - m_new)
            l = l + jnp.sum(p, axis=-1, keepdims=True)
            acc = acc + jnp.dot(p.astype(BF16), v, preferred_element_type=F32)
        m_scr[...] = m_new
        l_scr[...] = l
        acc_scr[...] = acc

    bias = bias_ref[...]
    s_list, v_list = [], []
    for g in range(pages):
        k2 = k_refs[g][...].reshape(cols, DIFF_V_DIM).astype(BF16)
        v2 = v_refs[g][...].reshape(cols, DIFF_V_DIM).astype(BF16)
        s = lax.dot_general(wq, k2, (((1,), (1,)), ((), ())), preferred_element_type=F32)
        s_list.append(s + bias)
        v_list.append(v2)
    online(s_list, v_list)

    @pl.when(step == n_steps - 1)
    def _():
        s = lax.dot_general(wq, kn_ref[0], (((1,), (1,)), ((), ())), preferred_element_type=F32)
        online([s + nbias_ref[...]], [vn_ref[0]])
        a = acc_scr[...] / l_scr[...]
        lam = lam_ref[0]
        outs = []
        for h in range(DIFF_HEADS):
            r0 = h * 2 * n_new
            o = a[r0:r0 + n_new] - lam * a[r0 + n_new:r0 + 2 * n_new]
            ms = jnp.mean(o * o, axis=-1, keepdims=True)
            outs.append(o * lax.rsqrt(ms + EPS) * g_ref[...] * post_scale)
        o_ref[0] = jnp.concatenate(outs, axis=-1)


def _sample_biases(n_new):
    rows = 2 * DIFF_HEADS * n_new
    r = np.arange(rows)[:, None]
    c = np.arange(PAGE_SIZE * DIFF_HEADS)[None, :]
    bias = np.where((r // (2 * n_new)) == (c % DIFF_HEADS), 0.0, NEG_BIG)
    c = np.arange(LANES)[None, :]
    ok = ((r // (2 * n_new)) == (c % DIFF_HEADS)) & ((c // DIFF_HEADS) <= (r % n_new)) \
        & (c < n_new * DIFF_HEADS)
    return jnp.asarray(bias, F32), jnp.asarray(np.where(ok, 0.0, NEG_BIG), F32)


def attn_sample(wq, k_new, v_new, lam, g, cache_k, cache_v, page_table, layer, pages, post_scale):
    batch, n_pages = page_table.shape
    n_new = wq.shape[1] // (2 * DIFF_HEADS)
    n_steps = n_pages // pages
    kern = functools.partial(_attn_sample_kernel, pages=pages, n_steps=n_steps, n_new=n_new,
                             post_scale=post_scale)
    bias, nbias = _sample_biases(n_new)

    def page_spec(gi):
        return pl.BlockSpec((None, None, PAGE_SIZE, DIFF_HEADS, DIFF_V_DIM),
                            lambda b, p, pt: (layer, pt[b, p * pages + gi], 0, 0, 0))

    per_b = lambda b, p, pt: (b, 0, 0)
    c2 = lambda b, p, pt: (0, 0)
    rows = wq.shape[1]
    grid_spec = pltpu.PrefetchScalarGridSpec(
        num_scalar_prefetch=1,
        grid=(batch, n_steps),
        in_specs=[
            pl.BlockSpec(memory_space=pltpu.SMEM),
            pl.BlockSpec((1, rows, DIFF_V_DIM), per_b),
            pl.BlockSpec((1,) + k_new.shape[1:], per_b),
            pl.BlockSpec((1,) + v_new.shape[1:], per_b),
            pl.BlockSpec((1, DIFF_V_DIM), c2),
            pl.BlockSpec(bias.shape, c2),
            pl.BlockSpec(nbias.shape, c2),
        ] + [page_spec(gi) for gi in range(pages)] * 2,
        out_specs=pl.BlockSpec((1, n_new, DIFF_W), per_b),
        scratch_shapes=[
            pltpu.VMEM((rows, 1), F32),
            pltpu.VMEM((rows, 1), F32),
            pltpu.VMEM((rows, DIFF_V_DIM), F32),
        ],
    )
    return pl.pallas_call(
        kern,
        grid_spec=grid_spec,
        out_shape=jax.ShapeDtypeStruct((batch, n_new, DIFF_W), F32),
        compiler_params=_cparams(("arbitrary", "arbitrary")),
        name="attn_sample",
    )(page_table, lam.reshape(1), wq, k_new, v_new, g.reshape(1, DIFF_V_DIM), bias, nbias,
      *([cache_k] * pages), *([cache_v] * pages))


def _merge_kernel(x_ref, ro_ref, do_ref, lo_ref, gz_ref, wb_ref, wo_ref, y_ref):
    mix = None
    for n, br in enumerate((ro_ref, do_ref, lo_ref)):
        proj = jnp.dot(br[...].astype(BF16), wb_ref[n], preferred_element_type=F32)
        gz = gz_ref[:, n * D_MODEL:(n + 1) * D_MODEL].astype(F32)
        term = jax.nn.sigmoid(gz) * proj
        mix = term if mix is None else mix + term
    y_ref[...] = x_ref[...] + jnp.dot(mix.astype(BF16), wo_ref[...], preferred_element_type=F32)


def merge(x, ro, do, lo, gz, wb_bf, wo_bf, tm):
    m = x.shape[0]
    row = lambda i: (i, 0)
    return pl.pallas_call(
        _merge_kernel,
        grid=(m // tm,),
        in_specs=[
            pl.BlockSpec((tm, D_MODEL), row),
            pl.BlockSpec((tm, D_MODEL), row),
            pl.BlockSpec((tm, D_MODEL), row),
            pl.BlockSpec((tm, D_MODEL), row),
            pl.BlockSpec((tm, N_BRANCH * D_MODEL), row),
            _resident((N_BRANCH, D_MODEL, D_MODEL), lambda i: (0, 0, 0)),
            _resident((D_MODEL, D_MODEL), lambda i: (0, 0)),
        ],
        out_specs=pl.BlockSpec((tm, D_MODEL), row),
        out_shape=jax.ShapeDtypeStruct((m, D_MODEL), F32),
        compiler_params=_cparams(("arbitrary",)),
        name="merge",
    )(x, ro, do, lo, gz, wb_bf, wo_bf)


FF_CHUNK = D_FF // 2


def _ffn_kernel(x_ref, g_ref, wg_ref, wu_ref, wd_ref, fg_ref, y_ref, *, final_norm):
    x = x_ref[...]
    ms = jnp.mean(x * x, axis=-1, keepdims=True)
    h = (x * lax.rsqrt(ms + EPS) * g_ref[...]).astype(BF16)
    y = x
    for c in range(D_FF // FF_CHUNK):
        cs = slice(c * FF_CHUNK, (c + 1) * FF_CHUNK)
        gate = jnp.dot(h, wg_ref[:, cs], preferred_element_type=F32)
        up = jnp.dot(h, wu_ref[:, cs], preferred_element_type=F32)
        act = (gate * jax.nn.sigmoid(gate) * up).astype(BF16)
        y = y + jnp.dot(act, wd_ref[cs, :], preferred_element_type=F32)
    if final_norm:
        ms = jnp.mean(y * y, axis=-1, keepdims=True)
        y = y * lax.rsqrt(ms + EPS) * fg_ref[...]
    y_ref[...] = y


def ffn(x, g, wg_bf, wu_bf, wd_bf, final_g, tm, final_norm):
    m = x.shape[0]
    c2 = lambda i: (0, 0)
    return pl.pallas_call(
        functools.partial(_ffn_kernel, final_norm=final_norm),
        grid=(m // tm,),
        in_specs=[
            pl.BlockSpec((tm, D_MODEL), lambda i: (i, 0)),
            pl.BlockSpec((1, D_MODEL), c2),
            _resident((D_MODEL, D_FF), c2),
            _resident((D_MODEL, D_FF), c2),
            _resident((D_FF, D_MODEL), c2),
            pl.BlockSpec((1, D_MODEL), c2),
        ],
        out_specs=pl.BlockSpec((tm, D_MODEL), lambda i: (i, 0)),
        out_shape=jax.ShapeDtypeStruct((m, D_MODEL), F32),
        compiler_params=_cparams(("arbitrary",)),
        name="ffn",
    )(x, g.reshape(1, D_MODEL), wg_bf, wu_bf, wd_bf, final_g.reshape(1, D_MODEL))


def _sample_query_blocks(dq, n_new):
    b = dq.shape[0] // n_new
    q = dq.reshape(b, n_new, DIFF_HEADS, 2, DIFF_HEAD_DIM).transpose(0, 2, 3, 1, 4)
    eye = jnp.eye(2, dtype=q.dtype)
    blk = q[:, :, :, :, None, :] * eye[None, None, :, None, :, None]
    return blk.reshape(b, DIFF_HEADS * 2 * n_new, 2 * DIFF_HEAD_DIM)


def _new_token_rows(x, n_new):
    b = x.shape[0] // n_new
    r = x.reshape(b, n_new * DIFF_HEADS, DIFF_V_DIM)
    return jnp.pad(r, ((0, 0), (0, LANES - n_new * DIFF_HEADS), (0, 0)))


def kernel(x_prompt, x_sample, cache_k, cache_v, page_table, state_ret, state_conv, state_lru,
           attn_norm_g, w_in, ret_gn_g, diff_lambda, diff_ln_g, conv_w, conv_b,
           gate_a_w, gate_a_b, gate_x_w, gate_x_b, lru_lambda, w_branch, w_out,
           ffn_norm_g, w_gate, w_up, w_down, final_norm_g):
    bp, seq, _ = x_prompt.shape
    bs, n_new, _ = x_sample.shape
    depth = w_in.shape[0]
    past = page_table.shape[1] * PAGE_SIZE
    pos_p = jnp.arange(seq)
    pos_s = past + jnp.arange(n_new)

    xp = x_prompt.reshape(bp * seq, D_MODEL)
    xs = x_sample.reshape(bs * n_new, D_MODEL)
    ret0 = jnp.zeros((bp, RET_HEADS, RET_QK_DIM, RET_V_DIM), F32)
    conv0 = jnp.zeros((bp, CONV_W - 1, LRU_W), F32)
    h00 = jnp.zeros((bp, LRU_W), F32)

    tm_s = bs * n_new
    heads = lambda a, b, n: a.reshape(b, n, DIFF_HEADS, DIFF_V_DIM)
    outs = {k: [] for k in ("kp", "vp", "rp", "cp", "hp", "ks", "vs", "rs", "cs", "hs")}
    for l in range(depth):
        lam_init = 0.8 - 0.6 * math.exp(-0.3 * l)
        post = 1.0 - lam_init
        lq1, lk1, lq2, lk2 = diff_lambda[l].astype(F32)
        lam = jnp.exp(jnp.sum(lq1 * lk1)) - jnp.exp(jnp.sum(lq2 * lk2)) + lam_init
        w_in_bf = w_in[l].astype(BF16)
        wb_bf = w_branch[l].astype(BF16)
        wo_bf = w_out[l].astype(BF16)
        wg_bf = w_gate[l].astype(BF16)
        wu_bf = w_up[l].astype(BF16)
        wd_bf = w_down[l].astype(BF16)
        wgate_bf = jnp.concatenate([gate_a_w[l], gate_x_w[l]], axis=-1).astype(BF16)
        last = l == depth - 1

        zr, q, k, vt, kf, vf, lx, gz = in_proj(xp, attn_norm_g[l], w_in_bf, TM_PROJ, True, BF16)
        ro, r1 = retention(zr, pos_p, ret0, ret_gn_g[l], bp, seq, RET_CHUNK, RET_CHUNK, BF16)
        do = attn_prompt(q, k, vt, lam, diff_ln_g[l], seq, ATT_TQ, ATT_TK, post)
        lo, c1, h1 = conv_lru(lx, conv0, h00, conv_w[l], conv_b[l], wgate_bf, gate_a_b[l],
                              gate_x_b[l], lru_lambda[l], bp, seq, LRU_CHUNK, BF16)
        xp = merge(xp, ro, do, lo, gz, wb_bf, wo_bf, TM_DENSE)
        xp = ffn(xp, ffn_norm_g[l], wg_bf, wu_bf, wd_bf, final_norm_g, TM_DENSE, last)
        outs["kp"].append(heads(kf, bp, seq))
        outs["vp"].append(heads(vf, bp, seq))
        outs["rp"].append(r1)
        outs["cp"].append(c1)
        outs["hp"].append(h1.reshape(bp, LRU_W))

        zr, q, k, v, kf, vf, lx, gz = in_proj(xs, attn_norm_g[l], w_in_bf, tm_s, False, F32)
        ro, r2 = retention(zr, pos_s, state_ret[l], ret_gn_g[l], bs, n_new, n_new, LANES, F32)
        wq = _sample_query_blocks(q, n_new)
        do = attn_sample(wq, _new_token_rows(k, n_new), _new_token_rows(v, n_new), lam, diff_ln_g[l],
                         cache_k, cache_v, page_table, l, PAGES_PER_STEP, post)
        do = do.reshape(bs * n_new, DIFF_W)
        lo, c2, h2 = conv_lru(lx, state_conv[l], state_lru[l], conv_w[l], conv_b[l], wgate_bf,
                              gate_a_b[l], gate_x_b[l], lru_lambda[l], bs, n_new, n_new, F32)
        xs = merge(xs, ro, do, lo, gz, wb_bf, wo_bf, tm_s)
        xs = ffn(xs, ffn_norm_g[l], wg_bf, wu_bf, wd_bf, final_norm_g, tm_s, last)
        outs["ks"].append(heads(kf, bs, n_new))
        outs["vs"].append(heads(vf, bs, n_new))
        outs["rs"].append(r2)
        outs["cs"].append(c2)
        outs["hs"].append(h2.reshape(bs, LRU_W))

    st = lambda k: jnp.stack(outs[k])
    return (xp.reshape(bp, seq, D_MODEL), xs.reshape(bs, n_new, D_MODEL),
            st("kp"), st("vp"), st("rp"), st("cp"), st("hp"),
            st("ks"), st("vs"), st("rs"), st("cs"), st("hs"))
```

```python
import functools
import math

import jax
import jax.numpy as jnp
import numpy as np
from jax import lax
from jax.experimental import pallas as pl
from jax.experimental.pallas import tpu as pltpu

F32 = jnp.float32
BF16 = jnp.bfloat16

D_MODEL = 1024
PAGE_SIZE = 128
RET_HEADS = 4
RET_QK_DIM = 128
RET_V_DIM = 256
RET_QK_W = RET_HEADS * RET_QK_DIM
RET_V_W = RET_HEADS * RET_V_DIM
ROPE_BASE = 10000.0
DIFF_HEADS = 8
DIFF_HEAD_DIM = 64
DIFF_V_DIM = 2 * DIFF_HEAD_DIM
DIFF_W = DIFF_HEADS * DIFF_V_DIM
LRU_W = D_MODEL
LRU_BLOCKS = 4
LRU_BLOCK_W = LRU_W // LRU_BLOCKS
CONV_W = 4
LRU_C = 8.0
N_BRANCH = 3
D_FF = 2816
IN_W = 10240
EPS = 1e-6
LOG2E = 1.4426950408889634

COL_DQ, COL_DK, COL_DV, COL_LX, COL_GZ = 3, 4, 5, 6, 7

LANES = 128
SUBLANES = 8
BF16_ROWS = 16
VMEM_LIMIT = 56 * 1024 * 1024
NEG_BIG = -1e30

TM_PROJ = 256
TM_DENSE = 512
RET_CHUNK = 256
LRU_CHUNK = 256
ATT_TQ = 512
ATT_TK = 512
ATT_HEADS = 2
PAGES_PER_STEP = 8
PAGE_GROUP = 4


def _cparams(sem):
    return pltpu.CompilerParams(dimension_semantics=sem, vmem_limit_bytes=VMEM_LIMIT)


def _resident(shape, index_map):
    return pl.BlockSpec(shape, index_map, pipeline_mode=pl.Buffered(1))


def _in_proj_kernel(x_ref, g_ref, w_ref, zr_ref, q_ref, k_ref, v_ref, kf_ref, vf_ref, lx_ref, gz_ref,
                    *, v_transposed):
    x = x_ref[...]
    ms = jnp.mean(x * x, axis=-1, keepdims=True)
    h = (x * lax.rsqrt(ms + EPS) * g_ref[...]).astype(BF16)

    def group(j):
        return jnp.dot(h, w_ref[:, j * 1024:(j + 1) * 1024], preferred_element_type=F32)

    for j in range(3):
        zr_ref[:, j * 1024:(j + 1) * 1024] = group(j).astype(zr_ref.dtype)
    q_ref[...] = (group(COL_DQ) * (LOG2E * DIFF_HEAD_DIM ** -0.5)).astype(BF16)
    zk = group(COL_DK)
    kf_ref[...] = zk
    k_ref[...] = zk.astype(BF16)
    zv = group(COL_DV)
    vf_ref[...] = zv
    if v_transposed:
        v_ref[0] = zv.T.astype(BF16)
    else:
        v_ref[...] = zv.astype(BF16)
    lx_ref[...] = group(COL_LX)
    for j in range(3):
        gz_ref[:, j * 1024:(j + 1) * 1024] = group(COL_GZ + j).astype(BF16)


def in_proj(x, g, w_bf, tm, v_transposed, zr_dtype):
    m = x.shape[0]
    row = lambda i: (i, 0)
    wide = lambda w, dt: (pl.BlockSpec((tm, w), row), jax.ShapeDtypeStruct((m, w), dt))
    if v_transposed:
        v_out = (pl.BlockSpec((1, DIFF_W, tm), lambda i: (i, 0, 0)),
                 jax.ShapeDtypeStruct((m // tm, DIFF_W, tm), BF16))
    else:
        v_out = wide(DIFF_W, BF16)
    outs = [wide(3 * 1024, zr_dtype), wide(DIFF_W, BF16), wide(DIFF_W, BF16), v_out,
            wide(DIFF_W, F32), wide(DIFF_W, F32), wide(LRU_W, F32), wide(3 * 1024, BF16)]
    return pl.pallas_call(
        functools.partial(_in_proj_kernel, v_transposed=v_transposed),
        grid=(m // tm,),
        in_specs=[
            pl.BlockSpec((tm, D_MODEL), row),
            pl.BlockSpec((1, D_MODEL), lambda i: (0, 0)),
            _resident((D_MODEL, IN_W), lambda i: (0, 0)),
        ],
        out_specs=[o[0] for o in outs],
        out_shape=[o[1] for o in outs],
        compiler_params=_cparams(("arbitrary",)),
        name="in_proj",
    )(x, g.reshape(1, D_MODEL), w_bf)


def _retention_tables(pos, c, cp):
    half = RET_QK_DIM // 2
    inv = ROPE_BASE ** (-jnp.arange(half, dtype=F32) / half)
    ang = pos.astype(F32)[:, None] * inv[None, :]
    cos, sin = jnp.cos(ang), jnp.sin(ang)
    cos2 = jnp.concatenate([cos, cos], axis=-1)
    sin2 = jnp.concatenate([-sin, sin], axis=-1)
    lg = np.log1p(-(2.0 ** (-5.0 - np.arange(RET_HEADS, dtype=np.float64))))
    idx = np.arange(cp, dtype=np.float64)
    rel = idx[:, None] - idx[None, :]
    ok = (rel >= 0) & (idx[:, None] < c) & (idx[None, :] < c)
    dmask = np.where(ok[None], np.exp(lg[:, None, None] * np.maximum(rel, 0.0)[None]), 0.0)
    valid = (idx < c)[:, None]
    rd = np.where(valid, np.exp((idx[:, None] + 1.0) * lg[None, :]), 0.0)
    wd = np.where(valid, np.exp((c - 1.0 - idx)[:, None] * lg[None, :]), 0.0)
    wd = wd * (RET_QK_DIM ** -0.5)
    rd = np.repeat(rd, RET_QK_DIM, axis=1)
    wd = np.repeat(wd, RET_QK_DIM, axis=1)
    cd = np.repeat(np.exp(c * lg), RET_V_DIM)[None, :]
    return (cos2, sin2, jnp.asarray(dmask, F32), jnp.asarray(rd, F32), jnp.asarray(wd, F32),
            jnp.asarray(cd, F32))


def _rope(x, cos2, sin2):
    return x * cos2 + pltpu.roll(x, RET_QK_DIM // 2, axis=1) * sin2


def _retention_kernel(rq_ref, rk_ref, rv_ref, rg_ref, cos_ref, sin_ref, dm_ref, rd_ref, wd_ref,
                      cd_ref, gn_ref, s0_ref, ro_ref, s_out_ref, s_scr, *, c, cp, n_chunks):
    ci = pl.program_id(1)

    @pl.when(ci == 0)
    def _():
        s_scr[...] = s0_ref[0]

    def pad(a):
        if c == cp:
            return a
        return jnp.concatenate([a, jnp.zeros((cp - c, a.shape[1]), a.dtype)], axis=0)

    cos2 = pad(cos_ref[...])
    sin2 = pad(sin_ref[...])
    rq = pad(rq_ref[...].astype(F32))
    rk = pad(rk_ref[...].astype(F32))
    rv = pad(rv_ref[...].astype(F32))
    rd = rd_ref[...]
    wd = wd_ref[...]
    outs = []
    for h in range(RET_HEADS):
        qs = slice(h * RET_QK_DIM, (h + 1) * RET_QK_DIM)
        vs = slice(h * RET_V_DIM, (h + 1) * RET_V_DIM)
        q = _rope(rq[:, qs], cos2, sin2)
        k = _rope(rk[:, qs], cos2, sin2)
        v = rv[:, vs].astype(BF16)
        s_old = s_scr[h]
        sc = lax.dot_general(q.astype(BF16), (k * (RET_QK_DIM ** -0.5)).astype(BF16),
                             (((1,), (1,)), ((), ())), preferred_element_type=F32)
        sc = sc * dm_ref[h]
        o = jnp.dot(sc.astype(BF16), v, preferred_element_type=F32)
        o = o + jnp.dot((q * rd[:, qs]).astype(BF16), s_old.astype(BF16), preferred_element_type=F32)
        kw = (k * wd[:, qs]).T.astype(BF16)
        s_scr[h] = s_old * cd_ref[:, vs] + jnp.dot(kw, v, preferred_element_type=F32)
        mu = jnp.mean(o, axis=-1, keepdims=True)
        d = o - mu
        var = jnp.mean(d * d, axis=-1, keepdims=True)
        outs.append(d * lax.rsqrt(var + EPS))
    o = jnp.concatenate(outs, axis=-1)[:c]
    rg = rg_ref[...].astype(F32)
    ro_ref[...] = (o * gn_ref[...] * (rg * jax.nn.sigmoid(rg))).astype(ro_ref.dtype)

    @pl.when(ci == n_chunks - 1)
    def _():
        s_out_ref[0] = s_scr[...]


def retention(zr, pos, s0, gn_g, batch, length, c, cp, out_dtype):
    n_chunks = length // c
    cos2, sin2, dmask, rd, wd, cd = _retention_tables(pos, c, cp)
    kern = functools.partial(_retention_kernel, c=c, cp=cp, n_chunks=n_chunks)
    row = lambda b, i: b * n_chunks + i
    const2 = lambda b, i: (0, 0)
    return pl.pallas_call(
        kern,
        grid=(batch, n_chunks),
        in_specs=[
            pl.BlockSpec((c, RET_QK_W), lambda b, i: (row(b, i), 0)),
            pl.BlockSpec((c, RET_QK_W), lambda b, i: (row(b, i), 1)),
            pl.BlockSpec((c, RET_V_W), lambda b, i: (row(b, i), 1)),
            pl.BlockSpec((c, RET_V_W), lambda b, i: (row(b, i), 2)),
            pl.BlockSpec((c, RET_QK_DIM), lambda b, i: (i, 0)),
            pl.BlockSpec((c, RET_QK_DIM), lambda b, i: (i, 0)),
            pl.BlockSpec((RET_HEADS, cp, cp), lambda b, i: (0, 0, 0)),
            pl.BlockSpec((cp, RET_QK_W), const2),
            pl.BlockSpec((cp, RET_QK_W), const2),
            pl.BlockSpec((1, RET_V_W), const2),
            pl.BlockSpec((1, RET_V_W), const2),
            pl.BlockSpec((1, RET_HEADS, RET_QK_DIM, RET_V_DIM), lambda b, i: (b, 0, 0, 0)),
        ],
        out_specs=[
            pl.BlockSpec((c, RET_V_W), lambda b, i: (row(b, i), 0)),
            pl.BlockSpec((1, RET_HEADS, RET_QK_DIM, RET_V_DIM), lambda b, i: (b, 0, 0, 0)),
        ],
        out_shape=[
            jax.ShapeDtypeStruct((batch * length, RET_V_W), out_dtype),
            jax.ShapeDtypeStruct((batch, RET_HEADS, RET_QK_DIM, RET_V_DIM), F32),
        ],
        scratch_shapes=[pltpu.VMEM((RET_HEADS, RET_QK_DIM, RET_V_DIM), F32)],
        compiler_params=_cparams(("arbitrary", "arbitrary")),
        name="retention",
    )(zr, zr, zr, zr, cos2, sin2, dmask, rd, wd, cd, gn_g.reshape(1, RET_V_W), s0)


def _lru_kernel(lx_ref, cbuf_ref, h0_ref, cw_ref, cb_ref, wg_ref, ba_ref, bx_ref, lam_ref,
                lo_ref, cnew_ref, hlast_ref, xe_scr, h_scr, a_scr, b_scr, *, t, n_chunks):
    ci = pl.program_id(1)
    keep = CONV_W - 1
    base = SUBLANES - keep

    @pl.when(ci == 0)
    def _():
        xe_scr[base:SUBLANES, :] = cbuf_ref[0]
        h_scr[...] = h0_ref[0]

    xe_scr[SUBLANES:SUBLANES + t, :] = lx_ref[...]
    xc = cb_ref[...] + xe_scr[base:base + t, :] * cw_ref[0:1, :]
    for j in range(1, CONV_W):
        xc = xc + xe_scr[base + j:base + j + t, :] * cw_ref[j:j + 1, :]
    tail = xe_scr[t + base:t + SUBLANES, :]
    xe_scr[base:SUBLANES, :] = tail

    sp = jax.nn.softplus(-lam_ref[...])
    for n in range(LRU_BLOCKS):
        ws = slice(n * LRU_BLOCK_W, (n + 1) * LRU_BLOCK_W)
        xb = xc[:, ws]
        gates = jnp.dot(xb.astype(BF16), wg_ref[n], preferred_element_type=F32)
        r = jax.nn.sigmoid(gates[:, :LRU_BLOCK_W] + ba_ref[:, ws])
        ig = jax.nn.sigmoid(gates[:, LRU_BLOCK_W:] + bx_ref[:, ws])
        log_a = (-LRU_C) * r * sp[:, ws]
        a = jnp.exp(log_a)
        a_scr[:, ws] = a
        b_scr[:, ws] = jnp.sqrt(1.0 - a * a) * (ig * xb)

    def body(i, h):
        r0 = pl.multiple_of(i * SUBLANES, SUBLANES)
        a8 = a_scr[pl.ds(r0, SUBLANES), :]
        b8 = b_scr[pl.ds(r0, SUBLANES), :]
        rows = []
        for s in range(SUBLANES):
            h = a8[s:s + 1, :] * h + b8[s:s + 1, :]
            rows.append(h)
        lo_ref[pl.ds(r0, SUBLANES), :] = jnp.concatenate(rows, axis=0).astype(lo_ref.dtype)
        return h

    h = lax.fori_loop(0, t // SUBLANES, body, h_scr[...])
    h_scr[...] = h

    @pl.when(ci == n_chunks - 1)
    def _():
        cnew_ref[0] = tail
        hlast_ref[0] = h


def conv_lru(lx, cbuf, h0, cw, cb, wg_bf, ba, bx, lam, batch, length, t, out_dtype):
    n_chunks = length // t
    kern = functools.partial(_lru_kernel, t=t, n_chunks=n_chunks)
    c2 = lambda b, i: (0, 0)
    return pl.pallas_call(
        kern,
        grid=(batch, n_chunks),
        in_specs=[
            pl.BlockSpec((t, LRU_W), lambda b, i: (b * n_chunks + i, 0)),
            pl.BlockSpec((1, CONV_W - 1, LRU_W), lambda b, i: (b, 0, 0)),
            pl.BlockSpec((1, 1, LRU_W), lambda b, i: (b, 0, 0)),
            pl.BlockSpec((CONV_W, LRU_W), c2),
            pl.BlockSpec((1, LRU_W), c2),
            pl.BlockSpec((LRU_BLOCKS, LRU_BLOCK_W, 2 * LRU_BLOCK_W), lambda b, i: (0, 0, 0)),
            pl.BlockSpec((1, LRU_W), c2),
            pl.BlockSpec((1, LRU_W), c2),
            pl.BlockSpec((1, LRU_W), c2),
        ],
        out_specs=[
            pl.BlockSpec((t, LRU_W), lambda b, i: (b * n_chunks + i, 0)),
            pl.BlockSpec((1, CONV_W - 1, LRU_W), lambda b, i: (b, 0, 0)),
            pl.BlockSpec((1, 1, LRU_W), lambda b, i: (b, 0, 0)),
        ],
        out_shape=[
            jax.ShapeDtypeStruct((batch * length, LRU_W), out_dtype),
            jax.ShapeDtypeStruct((batch, CONV_W - 1, LRU_W), F32),
            jax.ShapeDtypeStruct((batch, 1, LRU_W), F32),
        ],
        scratch_shapes=[
            pltpu.VMEM((t + SUBLANES, LRU_W), F32),
            pltpu.VMEM((1, LRU_W), F32),
            pltpu.VMEM((t, LRU_W), F32),
            pltpu.VMEM((t, LRU_W), F32),
        ],
        compiler_params=_cparams(("arbitrary", "arbitrary")),
        name="conv_lru",
    )(lx, cbuf, h0.reshape(batch, 1, LRU_W), cw, cb.reshape(1, LRU_W), wg_bf,
      ba.reshape(1, LRU_W), bx.reshape(1, LRU_W), lam.reshape(1, LRU_W))


def _attn_prompt_kernel(lam_ref, q_ref, k_ref, vt_ref, g_ref, bias_ref, o_ref, *, tq, tk, vt_tile,
                        hpb, post_scale):
    i = pl.program_id(1)
    half = DIFF_HEAD_DIM
    lane = lax.broadcasted_iota(jnp.int32, (tq, 2 * half), 1)
    ones = jnp.ones((BF16_ROWS, tk), BF16)
    per = tk // vt_tile
    streams = []
    for r in range(hpb):
        q = q_ref[:, r * DIFF_V_DIM:(r + 1) * DIFF_V_DIM]
        zero = jnp.zeros_like(q)
        streams.append((r, jnp.where(lane < half, q, zero)))
        streams.append((r, jnp.where(lane >= half, q, zero)))

    def update(j, carry, bias):
        row0 = pl.multiple_of(j * tk, tk)
        sts = []
        for r, qc in streams:
            k = k_ref[pl.ds(row0, tk), r * DIFF_V_DIM:(r + 1) * DIFF_V_DIM]
            st = lax.dot_general(k, qc, (((1,), (1,)), ((), ())), preferred_element_type=F32)
            sts.append(st if bias is None else st + bias)
        out = []
        for (r, _), st, (m, acc) in zip(streams, sts, carry):
            m_new = jnp.maximum(m, jnp.max(st, axis=0, keepdims=True))
            alpha = jnp.exp2(m - m_new)
            p = jnp.exp2(st - m_new).astype(BF16)
            va = jnp.concatenate([vt_ref[j * per + t, r * DIFF_V_DIM:(r + 1) * DIFF_V_DIM, :]
                                  for t in range(per)], axis=1)
            va = jnp.concatenate([va, ones], axis=0)
            out.append((m_new, alpha * acc + jnp.dot(va, p, preferred_element_type=F32)))
        return tuple(out)

    init = tuple((jnp.full((1, tq), NEG_BIG, F32), jnp.zeros((DIFF_V_DIM + BF16_ROWS, tq), F32))
                 for _ in streams)
    j_diag = (i * tq) // tk
    carry = lax.fori_loop(0, j_diag, lambda j, c: update(j, c, None), init)
    carry = update(j_diag, carry, bias_ref[...])
    for r in range(hpb):
        a1, a2 = (acc[:DIFF_V_DIM] / acc[DIFF_V_DIM:DIFF_V_DIM + 1] for _, acc in carry[2 * r:2 * r + 2])
        ot = a1 - lam_ref[0] * a2
        ms = jnp.mean(ot * ot, axis=0, keepdims=True)
        ot = ot * lax.rsqrt(ms + EPS)
        o_ref[:, r * DIFF_V_DIM:(r + 1) * DIFF_V_DIM] = (ot.T * g_ref[...] * post_scale).astype(o_ref.dtype)


def attn_prompt(q, k, vt, lam, g, seq, tq, tk, hpb, post_scale):
    assert tq == tk, "the diagonal tile is square"
    vt_tile = vt.shape[2]
    kern = functools.partial(_attn_prompt_kernel, tq=tq, tk=tk, vt_tile=vt_tile, hpb=hpb,
                             post_scale=post_scale)
    kk = np.arange(tk)[:, None]
    qq = np.arange(tq)[None, :]
    bias = jnp.asarray(np.where(kk <= qq, 0.0, NEG_BIG).astype(np.float32))
    w = hpb * DIFF_V_DIM
    return pl.pallas_call(
        kern,
        grid=(DIFF_HEADS // hpb, seq // tq),
        in_specs=[
            pl.BlockSpec(memory_space=pltpu.SMEM),
            pl.BlockSpec((tq, w), lambda h, i: (i, h)),
            pl.BlockSpec((seq, w), lambda h, i: (0, h)),
            pl.BlockSpec((seq // vt_tile, w, vt_tile), lambda h, i: (0, h, 0)),
            pl.BlockSpec((1, DIFF_V_DIM), lambda h, i: (0, 0)),
            pl.BlockSpec((tk, tq), lambda h, i: (0, 0)),
        ],
        out_specs=pl.BlockSpec((tq, w), lambda h, i: (i, h)),
        out_shape=jax.ShapeDtypeStruct((seq, DIFF_W), BF16),
        compiler_params=_cparams(("arbitrary", "arbitrary")),
        name="attn_prompt",
    )(lam.reshape(1), q, k, vt, g.reshape(1, DIFF_V_DIM), bias)


def _attn_sample_kernel(pt_ref, lam_ref, wq_ref, kn_ref, vn_ref, g_ref, bias_ref, nbias_ref, *rest,
                        pages, group, n_steps, n_new, post_scale):
    k_refs = rest[:pages]
    v_refs = rest[pages:2 * pages]
    o_ref = rest[2 * pages]
    m_scr, l_scr, acc_scr = rest[2 * pages + 1:]
    step = pl.program_id(1)
    cols = PAGE_SIZE * DIFF_HEADS

    @pl.when(step == 0)
    def _():
        m_scr[...] = jnp.full(m_scr.shape, NEG_BIG, F32)
        l_scr[...] = jnp.zeros(l_scr.shape, F32)
        acc_scr[...] = jnp.zeros(acc_scr.shape, F32)

    wq = wq_ref[0]

    def online(s_list, v_list):
        m_old = m_scr[...]
        m_new = m_old
        for s in s_list:
            m_new = jnp.maximum(m_new, jnp.max(s, axis=-1, keepdims=True))
        alpha = jnp.exp2(m_old - m_new)
        l = alpha * l_scr[...]
        acc = alpha * acc_scr[...]
        for s, v in zip(s_list, v_list):
            p = jnp.exp2(s - m_new)
            l = l + jnp.sum(p, axis=-1, keepdims=True)
            acc = acc + jnp.dot(p.astype(BF16), v, preferred_element_type=F32)
        m_scr[...] = m_new
        l_scr[...] = l
        acc_scr[...] = acc

    bias = bias_ref[...]

    def scores(g):
        k2 = k_refs[g][...].reshape(cols, DIFF_V_DIM).astype(BF16)
        s = lax.dot_general(wq, k2, (((1,), (1,)), ((), ())), preferred_element_type=F32)
        return s + bias

    def values(g):
        return v_refs[g][...].reshape(cols, DIFF_V_DIM).astype(BF16)

    groups = [list(range(g0, g0 + group)) for g0 in range(0, pages, group)]
    s_next = [scores(g) for g in groups[0]]
    for gi, grp in enumerate(groups):
        s_cur = s_next
        if gi + 1 < len(groups):
            s_next = [scores(g) for g in groups[gi + 1]]
        online(s_cur, [values(g) for g in grp])

    @pl.when(step == n_steps - 1)
    def _():
        s = lax.dot_general(wq, kn_ref[0], (((1,), (1,)), ((), ())), preferred_element_type=F32)
        online([s + nbias_ref[...]], [vn_ref[0]])
        a = acc_scr[...] / l_scr[...]
        lam = lam_ref[0]
        outs = []
        for h in range(DIFF_HEADS):
            r0 = h * 2 * n_new
            o = a[r0:r0 + n_new] - lam * a[r0 + n_new:r0 + 2 * n_new]
            ms = jnp.mean(o * o, axis=-1, keepdims=True)
            outs.append(o * lax.rsqrt(ms + EPS) * g_ref[...] * post_scale)
        o_ref[0] = jnp.concatenate(outs, axis=-1)


def _sample_biases(n_new):
    rows = 2 * DIFF_HEADS * n_new
    r = np.arange(rows)[:, None]
    c = np.arange(PAGE_SIZE * DIFF_HEADS)[None, :]
    bias = np.where((r // (2 * n_new)) == (c % DIFF_HEADS), 0.0, NEG_BIG)
    c = np.arange(LANES)[None, :]
    ok = ((r // (2 * n_new)) == (c % DIFF_HEADS)) & ((c // DIFF_HEADS) <= (r % n_new)) \
        & (c < n_new * DIFF_HEADS)
    return jnp.asarray(bias, F32), jnp.asarray(np.where(ok, 0.0, NEG_BIG), F32)


def attn_sample(wq, k_new, v_new, lam, g, cache_k, cache_v, page_table, layer, pages, post_scale):
    batch, n_pages = page_table.shape
    n_new = wq.shape[1] // (2 * DIFF_HEADS)
    n_steps = n_pages // pages
    kern = functools.partial(_attn_sample_kernel, pages=pages, group=PAGE_GROUP, n_steps=n_steps,
                             n_new=n_new, post_scale=post_scale)
    bias, nbias = _sample_biases(n_new)

    def page_spec(gi):
        return pl.BlockSpec((None, None, PAGE_SIZE, DIFF_HEADS, DIFF_V_DIM),
                            lambda b, p, pt: (layer, pt[b, p * pages + gi], 0, 0, 0))

    per_b = lambda b, p, pt: (b, 0, 0)
    c2 = lambda b, p, pt: (0, 0)
    rows = wq.shape[1]
    grid_spec = pltpu.PrefetchScalarGridSpec(
        num_scalar_prefetch=1,
        grid=(batch, n_steps),
        in_specs=[
            pl.BlockSpec(memory_space=pltpu.SMEM),
            pl.BlockSpec((1, rows, DIFF_V_DIM), per_b),
            pl.BlockSpec((1,) + k_new.shape[1:], per_b),
            pl.BlockSpec((1,) + v_new.shape[1:], per_b),
            pl.BlockSpec((1, DIFF_V_DIM), c2),
            pl.BlockSpec(bias.shape, c2),
            pl.BlockSpec(nbias.shape, c2),
        ] + [page_spec(gi) for gi in range(pages)] * 2,
        out_specs=pl.BlockSpec((1, n_new, DIFF_W), per_b),
        scratch_shapes=[
            pltpu.VMEM((rows, 1), F32),
            pltpu.VMEM((rows, 1), F32),
            pltpu.VMEM((rows, DIFF_V_DIM), F32),
        ],
    )
    return pl.pallas_call(
        kern,
        grid_spec=grid_spec,
        out_shape=jax.ShapeDtypeStruct((batch, n_new, DIFF_W), F32),
        compiler_params=_cparams(("arbitrary", "arbitrary")),
        name="attn_sample",
    )(page_table, lam.reshape(1), wq, k_new, v_new, g.reshape(1, DIFF_V_DIM), bias, nbias,
      *([cache_k] * pages), *([cache_v] * pages))


def _merge_kernel(x_ref, ro_ref, do_ref, lo_ref, gz_ref, wb_ref, wo_ref, y_ref):
    mix = None
    for n, br in enumerate((ro_ref, do_ref, lo_ref)):
        proj = jnp.dot(br[...].astype(BF16), wb_ref[n], preferred_element_type=F32)
        gz = gz_ref[:, n * D_MODEL:(n + 1) * D_MODEL].astype(F32)
        term = jax.nn.sigmoid(gz) * proj
        mix = term if mix is None else mix + term
    y_ref[...] = x_ref[...] + jnp.dot(mix.astype(BF16), wo_ref[...], preferred_element_type=F32)


def merge(x, ro, do, lo, gz, wb_bf, wo_bf, tm):
    m = x.shape[0]
    row = lambda i: (i, 0)
    return pl.pallas_call(
        _merge_kernel,
        grid=(m // tm,),
        in_specs=[
            pl.BlockSpec((tm, D_MODEL), row),
            pl.BlockSpec((tm, D_MODEL), row),
            pl.BlockSpec((tm, D_MODEL), row),
            pl.BlockSpec((tm, D_MODEL), row),
            pl.BlockSpec((tm, N_BRANCH * D_MODEL), row),
            _resident((N_BRANCH, D_MODEL, D_MODEL), lambda i: (0, 0, 0)),
            _resident((D_MODEL, D_MODEL), lambda i: (0, 0)),
        ],
        out_specs=pl.BlockSpec((tm, D_MODEL), row),
        out_shape=jax.ShapeDtypeStruct((m, D_MODEL), F32),
        compiler_params=_cparams(("arbitrary",)),
        name="merge",
    )(x, ro, do, lo, gz, wb_bf, wo_bf)


FF_CHUNK = D_FF // 2


def _ffn_kernel(x_ref, g_ref, wg_ref, wu_ref, wd_ref, fg_ref, y_ref, *, final_norm):
    x = x_ref[...]
    ms = jnp.mean(x * x, axis=-1, keepdims=True)
    h = (x * lax.rsqrt(ms + EPS) * g_ref[...]).astype(BF16)
    y = x
    for c in range(D_FF // FF_CHUNK):
        cs = slice(c * FF_CHUNK, (c + 1) * FF_CHUNK)
        gate = jnp.dot(h, wg_ref[:, cs], preferred_element_type=F32)
        up = jnp.dot(h, wu_ref[:, cs], preferred_element_type=F32)
        act = (gate * jax.nn.sigmoid(gate) * up).astype(BF16)
        y = y + jnp.dot(act, wd_ref[cs, :], preferred_element_type=F32)
    if final_norm:
        ms = jnp.mean(y * y, axis=-1, keepdims=True)
        y = y * lax.rsqrt(ms + EPS) * fg_ref[...]
    y_ref[...] = y


def ffn(x, g, wg_bf, wu_bf, wd_bf, final_g, tm, final_norm):
    m = x.shape[0]
    c2 = lambda i: (0, 0)
    return pl.pallas_call(
        functools.partial(_ffn_kernel, final_norm=final_norm),
        grid=(m // tm,),
        in_specs=[
            pl.BlockSpec((tm, D_MODEL), lambda i: (i, 0)),
            pl.BlockSpec((1, D_MODEL), c2),
            _resident((D_MODEL, D_FF), c2),
            _resident((D_MODEL, D_FF), c2),
            _resident((D_FF, D_MODEL), c2),
            pl.BlockSpec((1, D_MODEL), c2),
        ],
        out_specs=pl.BlockSpec((tm, D_MODEL), lambda i: (i, 0)),
        out_shape=jax.ShapeDtypeStruct((m, D_MODEL), F32),
        compiler_params=_cparams(("arbitrary",)),
        name="ffn",
    )(x, g.reshape(1, D_MODEL), wg_bf, wu_bf, wd_bf, final_g.reshape(1, D_MODEL))


def _sample_query_blocks(dq, n_new):
    b = dq.shape[0] // n_new
    q = dq.reshape(b, n_new, DIFF_HEADS, 2, DIFF_HEAD_DIM).transpose(0, 2, 3, 1, 4)
    eye = jnp.eye(2, dtype=q.dtype)
    blk = q[:, :, :, :, None, :] * eye[None, None, :, None, :, None]
    return blk.reshape(b, DIFF_HEADS * 2 * n_new, 2 * DIFF_HEAD_DIM)


def _new_token_rows(x, n_new):
    b = x.shape[0] // n_new
    r = x.reshape(b, n_new * DIFF_HEADS, DIFF_V_DIM)
    return jnp.pad(r, ((0, 0), (0, LANES - n_new * DIFF_HEADS), (0, 0)))


def kernel(x_prompt, x_sample, cache_k, cache_v, page_table, state_ret, state_conv, state_lru,
           attn_norm_g, w_in, ret_gn_g, diff_lambda, diff_ln_g, conv_w, conv_b,
           gate_a_w, gate_a_b, gate_x_w, gate_x_b, lru_lambda, w_branch, w_out,
           ffn_norm_g, w_gate, w_up, w_down, final_norm_g):
    bp, seq, _ = x_prompt.shape
    bs, n_new, _ = x_sample.shape
    depth = w_in.shape[0]
    past = page_table.shape[1] * PAGE_SIZE
    pos_p = jnp.arange(seq)
    pos_s = past + jnp.arange(n_new)

    xp = x_prompt.reshape(bp * seq, D_MODEL)
    xs = x_sample.reshape(bs * n_new, D_MODEL)
    ret0 = jnp.zeros((bp, RET_HEADS, RET_QK_DIM, RET_V_DIM), F32)
    conv0 = jnp.zeros((bp, CONV_W - 1, LRU_W), F32)
    h00 = jnp.zeros((bp, LRU_W), F32)

    tm_s = bs * n_new
    heads = lambda a, b, n: a.reshape(b, n, DIFF_HEADS, DIFF_V_DIM)
    outs = {k: [] for k in ("kp", "vp", "rp", "cp", "hp", "ks", "vs", "rs", "cs", "hs")}
    for l in range(depth):
        lam_init = 0.8 - 0.6 * math.exp(-0.3 * l)
        post = 1.0 - lam_init
        lq1, lk1, lq2, lk2 = diff_lambda[l].astype(F32)
        lam = jnp.exp(jnp.sum(lq1 * lk1)) - jnp.exp(jnp.sum(lq2 * lk2)) + lam_init
        w_in_bf = w_in[l].astype(BF16)
        wb_bf = w_branch[l].astype(BF16)
        wo_bf = w_out[l].astype(BF16)
        wg_bf = w_gate[l].astype(BF16)
        wu_bf = w_up[l].astype(BF16)
        wd_bf = w_down[l].astype(BF16)
        wgate_bf = jnp.concatenate([gate_a_w[l], gate_x_w[l]], axis=-1).astype(BF16)
        last = l == depth - 1

        zr, q, k, vt, kf, vf, lx, gz = in_proj(xp, attn_norm_g[l], w_in_bf, TM_PROJ, True, BF16)
        ro, r1 = retention(zr, pos_p, ret0, ret_gn_g[l], bp, seq, RET_CHUNK, RET_CHUNK, BF16)
        do = attn_prompt(q, k, vt, lam, diff_ln_g[l], seq, ATT_TQ, ATT_TK, ATT_HEADS, post)
        lo, c1, h1 = conv_lru(lx, conv0, h00, conv_w[l], conv_b[l], wgate_bf, gate_a_b[l],
                              gate_x_b[l], lru_lambda[l], bp, seq, LRU_CHUNK, BF16)
        xp = merge(xp, ro, do, lo, gz, wb_bf, wo_bf, TM_DENSE)
        xp = ffn(xp, ffn_norm_g[l], wg_bf, wu_bf, wd_bf, final_norm_g, TM_DENSE, last)
        outs["kp"].append(heads(kf, bp, seq))
        outs["vp"].append(heads(vf, bp, seq))
        outs["rp"].append(r1)
        outs["cp"].append(c1)
        outs["hp"].append(h1.reshape(bp, LRU_W))

        zr, q, k, v, kf, vf, lx, gz = in_proj(xs, attn_norm_g[l], w_in_bf, tm_s, False, F32)
        ro, r2 = retention(zr, pos_s, state_ret[l], ret_gn_g[l], bs, n_new, n_new, LANES, F32)
        wq = _sample_query_blocks(q, n_new)
        do = attn_sample(wq, _new_token_rows(k, n_new), _new_token_rows(v, n_new), lam, diff_ln_g[l],
                         cache_k, cache_v, page_table, l, PAGES_PER_STEP, post)
        do = do.reshape(bs * n_new, DIFF_W)
        lo, c2, h2 = conv_lru(lx, state_conv[l], state_lru[l], conv_w[l], conv_b[l], wgate_bf,
                              gate_a_b[l], gate_x_b[l], lru_lambda[l], bs, n_new, n_new, F32)
        xs = merge(xs, ro, do, lo, gz, wb_bf, wo_bf, tm_s)
        xs = ffn(xs, ffn_norm_g[l], wg_bf, wu_bf, wd_bf, final_norm_g, tm_s, last)
        outs["ks"].append(heads(kf, bs, n_new))
        outs["vs"].append(heads(vf, bs, n_new))
        outs["rs"].append(r2)
        outs["cs"].append(c2)
        outs["hs"].append(h2.reshape(bs, LRU_W))

    st = lambda k: jnp.stack(outs[k])
    return (xp.reshape(bp, seq, D_MODEL), xs.reshape(bs, n_new, D_MODEL),
            st("kp"), st("vp"), st("rp"), st("cp"), st("hp"),
            st("ks"), st("vs"), st("rs"), st("cs"), st("hs"))
```

```python
import functools
import math

import jax
import jax.numpy as jnp
import numpy as np
from jax import lax
from jax.experimental import pallas as pl
from jax.experimental.pallas import tpu as pltpu

F32 = jnp.float32
BF16 = jnp.bfloat16

D_MODEL = 1024
PAGE_SIZE = 128
RET_HEADS = 4
RET_QK_DIM = 128
RET_V_DIM = 256
RET_QK_W = RET_HEADS * RET_QK_DIM
RET_V_W = RET_HEADS * RET_V_DIM
ROPE_BASE = 10000.0
DIFF_HEADS = 8
DIFF_HEAD_DIM = 64
DIFF_V_DIM = 2 * DIFF_HEAD_DIM
DIFF_W = DIFF_HEADS * DIFF_V_DIM
LRU_W = D_MODEL
LRU_BLOCKS = 4
LRU_BLOCK_W = LRU_W // LRU_BLOCKS
CONV_W = 4
LRU_C = 8.0
N_BRANCH = 3
D_FF = 2816
IN_W = 10240
EPS = 1e-6
LOG2E = 1.4426950408889634

COL_DQ, COL_DK, COL_DV, COL_LX, COL_GZ = 3, 4, 5, 6, 7

LANES = 128
SUBLANES = 8
BF16_ROWS = 16
VMEM_LIMIT = 56 * 1024 * 1024
NEG_BIG = -1e30

TM_PROJ = 256
TM_DENSE = 512
RET_CHUNK = 256
LRU_CHUNK = 256
ATT_TQ = 512
ATT_TK = 1024
ATT_HEADS = 2
PAGES_PER_STEP = 8
PAGE_GROUP = 4


def _cparams(sem):
    return pltpu.CompilerParams(dimension_semantics=sem, vmem_limit_bytes=VMEM_LIMIT)


def _resident(shape, index_map):
    return pl.BlockSpec(shape, index_map, pipeline_mode=pl.Buffered(1))


def _in_proj_kernel(x_ref, g_ref, w_ref, *rest, v_transposed, n_alias):
    zr_ref, q_ref, k_ref, v_ref, kf_ref, vf_ref, lx_ref, gz_ref = rest[n_alias:]
    x = x_ref[...]
    tm = x.shape[0]
    ms = jnp.mean(x * x, axis=-1, keepdims=True)
    h = (x * lax.rsqrt(ms + EPS) * g_ref[...]).astype(BF16)

    def group(j):
        return jnp.dot(h, w_ref[:, j * 1024:(j + 1) * 1024], preferred_element_type=F32)

    def store_heads(ref, z):
        for hd in range(DIFF_HEADS):
            ref[pl.ds(hd, tm, stride=DIFF_HEADS), :] = z[:, hd * DIFF_V_DIM:(hd + 1) * DIFF_V_DIM]

    for j in range(3):
        zr_ref[:, j * 1024:(j + 1) * 1024] = group(j).astype(zr_ref.dtype)
    q_ref[...] = (group(COL_DQ) * (LOG2E * DIFF_HEAD_DIM ** -0.5)).astype(BF16)
    zk = group(COL_DK)
    store_heads(kf_ref, zk)
    k_ref[...] = zk.astype(BF16)
    zv = group(COL_DV)
    store_heads(vf_ref, zv)
    if v_transposed:
        v_ref[0] = zv.T.astype(BF16)
    else:
        v_ref[...] = zv.astype(BF16)
    lx_ref[...] = group(COL_LX)
    for j in range(3):
        gz_ref[:, j * 1024:(j + 1) * 1024] = group(COL_GZ + j).astype(BF16)


def in_proj(x, g, w_bf, tm, v_transposed, zr_dtype, layer, depth, kv_all):
    m = x.shape[0]
    row = lambda i: (i, 0)
    wide = lambda w, dt: (pl.BlockSpec((tm, w), row), jax.ShapeDtypeStruct((m, w), dt))
    if v_transposed:
        v_out = (pl.BlockSpec((1, DIFF_W, tm), lambda i: (i, 0, 0)),
                 jax.ShapeDtypeStruct((m // tm, DIFF_W, tm), BF16))
    else:
        v_out = wide(DIFF_W, BF16)
    slab = (pl.BlockSpec((None, tm * DIFF_HEADS, DIFF_V_DIM), lambda i: (layer, i, 0)),
            jax.ShapeDtypeStruct((depth, m * DIFF_HEADS, DIFF_V_DIM), F32))
    outs = [wide(3 * 1024, zr_dtype), wide(DIFF_W, BF16), wide(DIFF_W, BF16), v_out,
            slab, slab, wide(LRU_W, F32), wide(3 * 1024, BF16)]
    n_alias = len(kv_all)
    return pl.pallas_call(
        functools.partial(_in_proj_kernel, v_transposed=v_transposed, n_alias=n_alias),
        grid=(m // tm,),
        in_specs=[
            pl.BlockSpec((tm, D_MODEL), row),
            pl.BlockSpec((1, D_MODEL), lambda i: (0, 0)),
            _resident((D_MODEL, IN_W), lambda i: (0, 0)),
        ] + [pl.BlockSpec(memory_space=pl.ANY)] * n_alias,
        out_specs=[o[0] for o in outs],
        out_shape=[o[1] for o in outs],
        input_output_aliases={3 + a: 4 + a for a in range(n_alias)},
        compiler_params=_cparams(("arbitrary",)),
        name="in_proj",
    )(x, g.reshape(1, D_MODEL), w_bf, *kv_all)


def _retention_tables(pos, c, cp):
    half = RET_QK_DIM // 2
    inv = ROPE_BASE ** (-jnp.arange(half, dtype=F32) / half)
    ang = pos.astype(F32)[:, None] * inv[None, :]
    cos, sin = jnp.cos(ang), jnp.sin(ang)
    cos2 = jnp.concatenate([cos, cos], axis=-1)
    sin2 = jnp.concatenate([-sin, sin], axis=-1)
    lg = np.log1p(-(2.0 ** (-5.0 - np.arange(RET_HEADS, dtype=np.float64))))
    idx = np.arange(cp, dtype=np.float64)
    rel = idx[:, None] - idx[None, :]
    ok = (rel >= 0) & (idx[:, None] < c) & (idx[None, :] < c)
    dmask = np.where(ok[None], np.exp(lg[:, None, None] * np.maximum(rel, 0.0)[None]), 0.0)
    valid = (idx < c)[:, None]
    rd = np.where(valid, np.exp((idx[:, None] + 1.0) * lg[None, :]), 0.0)
    wd = np.where(valid, np.exp((c - 1.0 - idx)[:, None] * lg[None, :]), 0.0)
    wd = wd * (RET_QK_DIM ** -0.5)
    rd = np.repeat(rd, RET_QK_DIM, axis=1)
    wd = np.repeat(wd, RET_QK_DIM, axis=1)
    cd = np.repeat(np.exp(c * lg), RET_V_DIM)[None, :]
    return (cos2, sin2, jnp.asarray(dmask, F32), jnp.asarray(rd, F32), jnp.asarray(wd, F32),
            jnp.asarray(cd, F32))


def _rope(x, cos2, sin2):
    return x * cos2 + pltpu.roll(x, RET_QK_DIM // 2, axis=1) * sin2


def _retention_kernel(rq_ref, rk_ref, rv_ref, rg_ref, cos_ref, sin_ref, dm_ref, rd_ref, wd_ref,
                      cd_ref, gn_ref, s0_ref, ro_ref, s_out_ref, s_scr, *, c, cp, n_chunks):
    ci = pl.program_id(1)

    @pl.when(ci == 0)
    def _():
        s_scr[...] = s0_ref[0]

    def pad(a):
        if c == cp:
            return a
        return jnp.concatenate([a, jnp.zeros((cp - c, a.shape[1]), a.dtype)], axis=0)

    cos2 = pad(cos_ref[...])
    sin2 = pad(sin_ref[...])
    rq = pad(rq_ref[...].astype(F32))
    rk = pad(rk_ref[...].astype(F32))
    rv = pad(rv_ref[...].astype(F32))
    rd = rd_ref[...]
    wd = wd_ref[...]
    outs = []
    for h in range(RET_HEADS):
        qs = slice(h * RET_QK_DIM, (h + 1) * RET_QK_DIM)
        vs = slice(h * RET_V_DIM, (h + 1) * RET_V_DIM)
        q = _rope(rq[:, qs], cos2, sin2)
        k = _rope(rk[:, qs], cos2, sin2)
        v = rv[:, vs].astype(BF16)
        s_old = s_scr[h]
        sc = lax.dot_general(q.astype(BF16), (k * (RET_QK_DIM ** -0.5)).astype(BF16),
                             (((1,), (1,)), ((), ())), preferred_element_type=F32)
        sc = sc * dm_ref[h]
        o = jnp.dot(sc.astype(BF16), v, preferred_element_type=F32)
        o = o + jnp.dot((q * rd[:, qs]).astype(BF16), s_old.astype(BF16), preferred_element_type=F32)
        kw = (k * wd[:, qs]).T.astype(BF16)
        s_scr[h] = s_old * cd_ref[:, vs] + jnp.dot(kw, v, preferred_element_type=F32)
        mu = jnp.mean(o, axis=-1, keepdims=True)
        d = o - mu
        var = jnp.mean(d * d, axis=-1, keepdims=True)
        outs.append(d * lax.rsqrt(var + EPS))
    o = jnp.concatenate(outs, axis=-1)[:c]
    rg = rg_ref[...].astype(F32)
    ro_ref[...] = (o * gn_ref[...] * (rg * jax.nn.sigmoid(rg))).astype(ro_ref.dtype)

    @pl.when(ci == n_chunks - 1)
    def _():
        s_out_ref[0] = s_scr[...]


def retention(zr, pos, s0, gn_g, batch, length, c, cp, out_dtype):
    n_chunks = length // c
    cos2, sin2, dmask, rd, wd, cd = _retention_tables(pos, c, cp)
    kern = functools.partial(_retention_kernel, c=c, cp=cp, n_chunks=n_chunks)
    row = lambda b, i: b * n_chunks + i
    const2 = lambda b, i: (0, 0)
    return pl.pallas_call(
        kern,
        grid=(batch, n_chunks),
        in_specs=[
            pl.BlockSpec((c, RET_QK_W), lambda b, i: (row(b, i), 0)),
            pl.BlockSpec((c, RET_QK_W), lambda b, i: (row(b, i), 1)),
            pl.BlockSpec((c, RET_V_W), lambda b, i: (row(b, i), 1)),
            pl.BlockSpec((c, RET_V_W), lambda b, i: (row(b, i), 2)),
            pl.BlockSpec((c, RET_QK_DIM), lambda b, i: (i, 0)),
            pl.BlockSpec((c, RET_QK_DIM), lambda b, i: (i, 0)),
            pl.BlockSpec((RET_HEADS, cp, cp), lambda b, i: (0, 0, 0)),
            pl.BlockSpec((cp, RET_QK_W), const2),
            pl.BlockSpec((cp, RET_QK_W), const2),
            pl.BlockSpec((1, RET_V_W), const2),
            pl.BlockSpec((1, RET_V_W), const2),
            pl.BlockSpec((1, RET_HEADS, RET_QK_DIM, RET_V_DIM), lambda b, i: (b, 0, 0, 0)),
        ],
        out_specs=[
            pl.BlockSpec((c, RET_V_W), lambda b, i: (row(b, i), 0)),
            pl.BlockSpec((1, RET_HEADS, RET_QK_DIM, RET_V_DIM), lambda b, i: (b, 0, 0, 0)),
        ],
        out_shape=[
            jax.ShapeDtypeStruct((batch * length, RET_V_W), out_dtype),
            jax.ShapeDtypeStruct((batch, RET_HEADS, RET_QK_DIM, RET_V_DIM), F32),
        ],
        scratch_shapes=[pltpu.VMEM((RET_HEADS, RET_QK_DIM, RET_V_DIM), F32)],
        compiler_params=_cparams(("arbitrary", "arbitrary")),
        name="retention",
    )(zr, zr, zr, zr, cos2, sin2, dmask, rd, wd, cd, gn_g.reshape(1, RET_V_W), s0)


def _lru_kernel(lx_ref, cbuf_ref, h0_ref, cw_ref, cb_ref, wg_ref, ba_ref, bx_ref, lam_ref,
                lo_ref, cnew_ref, hlast_ref, xe_scr, h_scr, a_scr, b_scr, *, t, n_chunks):
    ci = pl.program_id(1)
    keep = CONV_W - 1
    base = SUBLANES - keep

    @pl.when(ci == 0)
    def _():
        xe_scr[base:SUBLANES, :] = cbuf_ref[0]
        h_scr[...] = h0_ref[0]

    xe_scr[SUBLANES:SUBLANES + t, :] = lx_ref[...]
    xc = cb_ref[...] + xe_scr[base:base + t, :] * cw_ref[0:1, :]
    for j in range(1, CONV_W):
        xc = xc + xe_scr[base + j:base + j + t, :] * cw_ref[j:j + 1, :]
    tail = xe_scr[t + base:t + SUBLANES, :]
    xe_scr[base:SUBLANES, :] = tail

    sp = jax.nn.softplus(-lam_ref[...])
    for n in range(LRU_BLOCKS):
        ws = slice(n * LRU_BLOCK_W, (n + 1) * LRU_BLOCK_W)
        xb = xc[:, ws]
        gates = jnp.dot(xb.astype(BF16), wg_ref[n], preferred_element_type=F32)
        r = jax.nn.sigmoid(gates[:, :LRU_BLOCK_W] + ba_ref[:, ws])
        ig = jax.nn.sigmoid(gates[:, LRU_BLOCK_W:] + bx_ref[:, ws])
        log_a = (-LRU_C) * r * sp[:, ws]
        a = jnp.exp(log_a)
        a_scr[:, ws] = a
        b_scr[:, ws] = jnp.sqrt(1.0 - a * a) * (ig * xb)

    def body(i, h):
        r0 = pl.multiple_of(i * SUBLANES, SUBLANES)
        a8 = a_scr[pl.ds(r0, SUBLANES), :]
        b8 = b_scr[pl.ds(r0, SUBLANES), :]
        rows = []
        for s in range(SUBLANES):
            h = a8[s:s + 1, :] * h + b8[s:s + 1, :]
            rows.append(h)
        lo_ref[pl.ds(r0, SUBLANES), :] = jnp.concatenate(rows, axis=0).astype(lo_ref.dtype)
        return h

    h = lax.fori_loop(0, t // SUBLANES, body, h_scr[...])
    h_scr[...] = h

    @pl.when(ci == n_chunks - 1)
    def _():
        cnew_ref[0] = tail
        hlast_ref[0] = h


def conv_lru(lx, cbuf, h0, cw, cb, wg_bf, ba, bx, lam, batch, length, t, out_dtype):
    n_chunks = length // t
    kern = functools.partial(_lru_kernel, t=t, n_chunks=n_chunks)
    c2 = lambda b, i: (0, 0)
    return pl.pallas_call(
        kern,
        grid=(batch, n_chunks),
        in_specs=[
            pl.BlockSpec((t, LRU_W), lambda b, i: (b * n_chunks + i, 0)),
            pl.BlockSpec((1, CONV_W - 1, LRU_W), lambda b, i: (b, 0, 0)),
            pl.BlockSpec((1, 1, LRU_W), lambda b, i: (b, 0, 0)),
            pl.BlockSpec((CONV_W, LRU_W), c2),
            pl.BlockSpec((1, LRU_W), c2),
            pl.BlockSpec((LRU_BLOCKS, LRU_BLOCK_W, 2 * LRU_BLOCK_W), lambda b, i: (0, 0, 0)),
            pl.BlockSpec((1, LRU_W), c2),
            pl.BlockSpec((1, LRU_W), c2),
            pl.BlockSpec((1, LRU_W), c2),
        ],
        out_specs=[
            pl.BlockSpec((t, LRU_W), lambda b, i: (b * n_chunks + i, 0)),
            pl.BlockSpec((1, CONV_W - 1, LRU_W), lambda b, i: (b, 0, 0)),
            pl.BlockSpec((1, 1, LRU_W), lambda b, i: (b, 0, 0)),
        ],
        out_shape=[
            jax.ShapeDtypeStruct((batch * length, LRU_W), out_dtype),
            jax.ShapeDtypeStruct((batch, CONV_W - 1, LRU_W), F32),
            jax.ShapeDtypeStruct((batch, 1, LRU_W), F32),
        ],
        scratch_shapes=[
            pltpu.VMEM((t + SUBLANES, LRU_W), F32),
            pltpu.VMEM((1, LRU_W), F32),
            pltpu.VMEM((t, LRU_W), F32),
            pltpu.VMEM((t, LRU_W), F32),
        ],
        compiler_params=_cparams(("arbitrary", "arbitrary")),
        name="conv_lru",
    )(lx, cbuf, h0.reshape(batch, 1, LRU_W), cw, cb.reshape(1, LRU_W), wg_bf,
      ba.reshape(1, LRU_W), bx.reshape(1, LRU_W), lam.reshape(1, LRU_W))


def _attn_prompt_kernel(lam_ref, q_ref, k_ref, vt_ref, g_ref, bias_ref, o_ref, *, tq, tk, vt_tile,
                        hpb, post_scale):
    i = pl.program_id(1)
    half = DIFF_HEAD_DIM
    lane = lax.broadcasted_iota(jnp.int32, (tq, 2 * half), 1)
    streams = []
    for r in range(hpb):
        q = q_ref[:, r * DIFF_V_DIM:(r + 1) * DIFF_V_DIM]
        zero = jnp.zeros_like(q)
        streams.append((r, jnp.where(lane < half, q, zero)))
        streams.append((r, jnp.where(lane >= half, q, zero)))

    def update(t0, nk, carry, bias):
        row0 = pl.multiple_of(t0 * tq, tq)
        v0 = t0 * (tq // vt_tile)
        ones = jnp.ones((BF16_ROWS, nk), BF16)
        sts = []
        for r, qc in streams:
            k = k_ref[pl.ds(row0, nk), r * DIFF_V_DIM:(r + 1) * DIFF_V_DIM]
            st = lax.dot_general(k, qc, (((1,), (1,)), ((), ())), preferred_element_type=F32)
            sts.append(st if bias is None else st + bias)
        out = []
        for (r, _), st, (m, acc) in zip(streams, sts, carry):
            m_new = jnp.maximum(m, jnp.max(st, axis=0, keepdims=True))
            alpha = jnp.exp2(m - m_new)
            p = jnp.exp2(st - m_new).astype(BF16)
            va = jnp.concatenate([vt_ref[v0 + t, r * DIFF_V_DIM:(r + 1) * DIFF_V_DIM, :]
                                  for t in range(nk // vt_tile)], axis=1)
            va = jnp.concatenate([va, ones], axis=0)
            out.append((m_new, alpha * acc + jnp.dot(va, p, preferred_element_type=F32)))
        return tuple(out)

    init = tuple((jnp.full((1, tq), NEG_BIG, F32), jnp.zeros((DIFF_V_DIM + BF16_ROWS, tq), F32))
                 for _ in streams)
    per = tk // tq
    carry = lax.fori_loop(0, i // per, lambda j, c: update(j * per, tk, c, None), init)
    carry = lax.cond(i % per == 1, lambda c: update(i - 1, tq, c, None), lambda c: c, carry)
    carry = update(i, tq, carry, bias_ref[...])
    for r in range(hpb):
        a1, a2 = (acc[:DIFF_V_DIM] / acc[DIFF_V_DIM:DIFF_V_DIM + 1] for _, acc in carry[2 * r:2 * r + 2])
        ot = a1 - lam_ref[0] * a2
        ms = jnp.mean(ot * ot, axis=0, keepdims=True)
        ot = ot * lax.rsqrt(ms + EPS)
        o_ref[:, r * DIFF_V_DIM:(r + 1) * DIFF_V_DIM] = (ot.T * g_ref[...] * post_scale).astype(o_ref.dtype)


def attn_prompt(q, k, vt, lam, g, seq, tq, tk, hpb, post_scale):
    assert tk == 2 * tq, "full key tiles are two query tiles long"
    vt_tile = vt.shape[2]
    kern = functools.partial(_attn_prompt_kernel, tq=tq, tk=tk, vt_tile=vt_tile, hpb=hpb,
                             post_scale=post_scale)
    kk = np.arange(tq)[:, None]
    qq = np.arange(tq)[None, :]
    bias = jnp.asarray(np.where(kk <= qq, 0.0, NEG_BIG).astype(np.float32))
    w = hpb * DIFF_V_DIM
    return pl.pallas_call(
        kern,
        grid=(DIFF_HEADS // hpb, seq // tq),
        in_specs=[
            pl.BlockSpec(memory_space=pltpu.SMEM),
            pl.BlockSpec((tq, w), lambda h, i: (i, h)),
            pl.BlockSpec((seq, w), lambda h, i: (0, h)),
            pl.BlockSpec((seq // vt_tile, w, vt_tile), lambda h, i: (0, h, 0)),
            pl.BlockSpec((1, DIFF_V_DIM), lambda h, i: (0, 0)),
            pl.BlockSpec((tq, tq), lambda h, i: (0, 0)),
        ],
        out_specs=pl.BlockSpec((tq, w), lambda h, i: (i, h)),
        out_shape=jax.ShapeDtypeStruct((seq, DIFF_W), BF16),
        compiler_params=_cparams(("arbitrary", "arbitrary")),
        name="attn_prompt",
    )(lam.reshape(1), q, k, vt, g.reshape(1, DIFF_V_DIM), bias)


def _attn_sample_kernel(pt_ref, lam_ref, wq_ref, kn_ref, vn_ref, g_ref, bias_ref, nbias_ref, *rest,
                        pages, group, n_steps, n_new, post_scale):
    k_refs = rest[:pages]
    v_refs = rest[pages:2 * pages]
    o_ref = rest[2 * pages]
    m_scr, l_scr, acc_scr = rest[2 * pages + 1:]
    step = pl.program_id(1)
    cols = PAGE_SIZE * DIFF_HEADS

    @pl.when(step == 0)
    def _():
        m_scr[...] = jnp.full(m_scr.shape, NEG_BIG, F32)
        l_scr[...] = jnp.zeros(l_scr.shape, F32)
        acc_scr[...] = jnp.zeros(acc_scr.shape, F32)

    wq = wq_ref[0]

    def online(s_list, v_list):
        m_old = m_scr[...]
        m_new = m_old
        for s in s_list:
            m_new = jnp.maximum(m_new, jnp.max(s, axis=-1, keepdims=True))
        alpha = jnp.exp2(m_old - m_new)
        l = alpha * l_scr[...]
        acc = alpha * acc_scr[...]
        for s, v in zip(s_list, v_list):
            p = jnp.exp2(s - m_new)
            l = l + jnp.sum(p, axis=-1, keepdims=True)
            acc = acc + jnp.dot(p.astype(BF16), v, preferred_element_type=F32)
        m_scr[...] = m_new
        l_scr[...] = l
        acc_scr[...] = acc

    bias = bias_ref[...]

    def scores(g):
        k2 = k_refs[g][...].reshape(cols, DIFF_V_DIM).astype(BF16)
        s = lax.dot_general(wq, k2, (((1,), (1,)), ((), ())), preferred_element_type=F32)
        return s + bias

    def values(g):
        return v_refs[g][...].reshape(cols, DIFF_V_DIM).astype(BF16)

    groups = [list(range(g0, g0 + group)) for g0 in range(0, pages, group)]
    s_next = [scores(g) for g in groups[0]]
    for gi, grp in enumerate(groups):
        s_cur = s_next
        if gi + 1 < len(groups):
            s_next = [scores(g) for g in groups[gi + 1]]
        online(s_cur, [values(g) for g in grp])

    @pl.when(step == n_steps - 1)
    def _():
        s = lax.dot_general(wq, kn_ref[0], (((1,), (1,)), ((), ())), preferred_element_type=F32)
        online([s + nbias_ref[...]], [vn_ref[0]])
        a = acc_scr[...] / l_scr[...]
        lam = lam_ref[0]
        outs = []
        for h in range(DIFF_HEADS):
            r0 = h * 2 * n_new
            o = a[r0:r0 + n_new] - lam * a[r0 + n_new:r0 + 2 * n_new]
            ms = jnp.mean(o * o, axis=-1, keepdims=True)
            outs.append(o * lax.rsqrt(ms + EPS) * g_ref[...] * post_scale)
        o_ref[0] = jnp.concatenate(outs, axis=-1)


def _sample_biases(n_new):
    rows = 2 * DIFF_HEADS * n_new
    r = np.arange(rows)[:, None]
    c = np.arange(PAGE_SIZE * DIFF_HEADS)[None, :]
    bias = np.where((r // (2 * n_new)) == (c % DIFF_HEADS), 0.0, NEG_BIG)
    c = np.arange(LANES)[None, :]
    ok = ((r // (2 * n_new)) == (c % DIFF_HEADS)) & ((c // DIFF_HEADS) <= (r % n_new)) \
        & (c < n_new * DIFF_HEADS)
    return jnp.asarray(bias, F32), jnp.asarray(np.where(ok, 0.0, NEG_BIG), F32)


def attn_sample(wq, k_new, v_new, lam, g, cache_k, cache_v, page_table, layer, pages, post_scale):
    batch, n_pages = page_table.shape
    n_new = wq.shape[1] // (2 * DIFF_HEADS)
    n_steps = n_pages // pages
    kern = functools.partial(_attn_sample_kernel, pages=pages, group=PAGE_GROUP, n_steps=n_steps,
                             n_new=n_new, post_scale=post_scale)
    bias, nbias = _sample_biases(n_new)

    def page_spec(gi):
        return pl.BlockSpec((None, None, PAGE_SIZE, DIFF_HEADS, DIFF_V_DIM),
                            lambda b, p, pt: (layer, pt[b, p * pages + gi], 0, 0, 0))

    per_b = lambda b, p, pt: (b, 0, 0)
    c2 = lambda b, p, pt: (0, 0)
    rows = wq.shape[1]
    grid_spec = pltpu.PrefetchScalarGridSpec(
        num_scalar_prefetch=1,
        grid=(batch, n_steps),
        in_specs=[
            pl.BlockSpec(memory_space=pltpu.SMEM),
            pl.BlockSpec((1, rows, DIFF_V_DIM), per_b),
            pl.BlockSpec((1,) + k_new.shape[1:], per_b),
            pl.BlockSpec((1,) + v_new.shape[1:], per_b),
            pl.BlockSpec((1, DIFF_V_DIM), c2),
            pl.BlockSpec(bias.shape, c2),
            pl.BlockSpec(nbias.shape, c2),
        ] + [page_spec(gi) for gi in range(pages)] * 2,
        out_specs=pl.BlockSpec((1, n_new, DIFF_W), per_b),
        scratch_shapes=[
            pltpu.VMEM((rows, 1), F32),
            pltpu.VMEM((rows, 1), F32),
            pltpu.VMEM((rows, DIFF_V_DIM), F32),
        ],
    )
    return pl.pallas_call(
        kern,
        grid_spec=grid_spec,
        out_shape=jax.ShapeDtypeStruct((batch, n_new, DIFF_W), F32),
        compiler_params=_cparams(("arbitrary", "arbitrary")),
        name="attn_sample",
    )(page_table, lam.reshape(1), wq, k_new, v_new, g.reshape(1, DIFF_V_DIM), bias, nbias,
      *([cache_k] * pages), *([cache_v] * pages))


def _merge_kernel(x_ref, ro_ref, do_ref, lo_ref, gz_ref, wb_ref, wo_ref, y_ref):
    mix = None
    for n, br in enumerate((ro_ref, do_ref, lo_ref)):
        proj = jnp.dot(br[...].astype(BF16), wb_ref[n], preferred_element_type=F32)
        gz = gz_ref[:, n * D_MODEL:(n + 1) * D_MODEL].astype(F32)
        term = jax.nn.sigmoid(gz) * proj
        mix = term if mix is None else mix + term
    y_ref[...] = x_ref[...] + jnp.dot(mix.astype(BF16), wo_ref[...], preferred_element_type=F32)


def merge(x, ro, do, lo, gz, wb_bf, wo_bf, tm):
    m = x.shape[0]
    row = lambda i: (i, 0)
    return pl.pallas_call(
        _merge_kernel,
        grid=(m // tm,),
        in_specs=[
            pl.BlockSpec((tm, D_MODEL), row),
            pl.BlockSpec((tm, D_MODEL), row),
            pl.BlockSpec((tm, D_MODEL), row),
            pl.BlockSpec((tm, D_MODEL), row),
            pl.BlockSpec((tm, N_BRANCH * D_MODEL), row),
            _resident((N_BRANCH, D_MODEL, D_MODEL), lambda i: (0, 0, 0)),
            _resident((D_MODEL, D_MODEL), lambda i: (0, 0)),
        ],
        out_specs=pl.BlockSpec((tm, D_MODEL), row),
        out_shape=jax.ShapeDtypeStruct((m, D_MODEL), F32),
        compiler_params=_cparams(("arbitrary",)),
        name="merge",
    )(x, ro, do, lo, gz, wb_bf, wo_bf)


FF_CHUNK = D_FF // 2


def _ffn_kernel(x_ref, g_ref, wg_ref, wu_ref, wd_ref, fg_ref, y_ref, *, final_norm):
    x = x_ref[...]
    ms = jnp.mean(x * x, axis=-1, keepdims=True)
    h = (x * lax.rsqrt(ms + EPS) * g_ref[...]).astype(BF16)
    y = x
    for c in range(D_FF // FF_CHUNK):
        cs = slice(c * FF_CHUNK, (c + 1) * FF_CHUNK)
        gate = jnp.dot(h, wg_ref[:, cs], preferred_element_type=F32)
        up = jnp.dot(h, wu_ref[:, cs], preferred_element_type=F32)
        act = (gate * jax.nn.sigmoid(gate) * up).astype(BF16)
        y = y + jnp.dot(act, wd_ref[cs, :], preferred_element_type=F32)
    if final_norm:
        ms = jnp.mean(y * y, axis=-1, keepdims=True)
        y = y * lax.rsqrt(ms + EPS) * fg_ref[...]
    y_ref[...] = y


def ffn(x, g, wg_bf, wu_bf, wd_bf, final_g, tm, final_norm):
    m = x.shape[0]
    c2 = lambda i: (0, 0)
    return pl.pallas_call(
        functools.partial(_ffn_kernel, final_norm=final_norm),
        grid=(m // tm,),
        in_specs=[
            pl.BlockSpec((tm, D_MODEL), lambda i: (i, 0)),
            pl.BlockSpec((1, D_MODEL), c2),
            _resident((D_MODEL, D_FF), c2),
            _resident((D_MODEL, D_FF), c2),
            _resident((D_FF, D_MODEL), c2),
            pl.BlockSpec((1, D_MODEL), c2),
        ],
        out_specs=pl.BlockSpec((tm, D_MODEL), lambda i: (i, 0)),
        out_shape=jax.ShapeDtypeStruct((m, D_MODEL), F32),
        compiler_params=_cparams(("arbitrary",)),
        name="ffn",
    )(x, g.reshape(1, D_MODEL), wg_bf, wu_bf, wd_bf, final_g.reshape(1, D_MODEL))


def _sample_query_blocks(dq, n_new):
    b = dq.shape[0] // n_new
    q = dq.reshape(b, n_new, DIFF_HEADS, 2, DIFF_HEAD_DIM).transpose(0, 2, 3, 1, 4)
    eye = jnp.eye(2, dtype=q.dtype)
    blk = q[:, :, :, :, None, :] * eye[None, None, :, None, :, None]
    return blk.reshape(b, DIFF_HEADS * 2 * n_new, 2 * DIFF_HEAD_DIM)


def _new_token_rows(x, n_new):
    b = x.shape[0] // n_new
    r = x.reshape(b, n_new * DIFF_HEADS, DIFF_V_DIM)
    return jnp.pad(r, ((0, 0), (0, LANES - n_new * DIFF_HEADS), (0, 0)))


def kernel(x_prompt, x_sample, cache_k, cache_v, page_table, state_ret, state_conv, state_lru,
           attn_norm_g, w_in, ret_gn_g, diff_lambda, diff_ln_g, conv_w, conv_b,
           gate_a_w, gate_a_b, gate_x_w, gate_x_b, lru_lambda, w_branch, w_out,
           ffn_norm_g, w_gate, w_up, w_down, final_norm_g):
    bp, seq, _ = x_prompt.shape
    bs, n_new, _ = x_sample.shape
    depth = w_in.shape[0]
    past = page_table.shape[1] * PAGE_SIZE
    pos_p = jnp.arange(seq)
    pos_s = past + jnp.arange(n_new)

    xp = x_prompt.reshape(bp * seq, D_MODEL)
    xs = x_sample.reshape(bs * n_new, D_MODEL)
    ret0 = jnp.zeros((bp, RET_HEADS, RET_QK_DIM, RET_V_DIM), F32)
    conv0 = jnp.zeros((bp, CONV_W - 1, LRU_W), F32)
    h00 = jnp.zeros((bp, LRU_W), F32)

    tm_s = bs * n_new
    heads = lambda a, b, n: a.reshape(depth, b, n, DIFF_HEADS, DIFF_V_DIM)
    outs = {k: [] for k in ("rp", "cp", "hp", "rs", "cs", "hs")}
    for l in range(depth):
        lam_init = 0.8 - 0.6 * math.exp(-0.3 * l)
        post = 1.0 - lam_init
        lq1, lk1, lq2, lk2 = diff_lambda[l].astype(F32)
        lam = jnp.exp(jnp.sum(lq1 * lk1)) - jnp.exp(jnp.sum(lq2 * lk2)) + lam_init
        w_in_bf = w_in[l].astype(BF16)
        wb_bf = w_branch[l].astype(BF16)
        wo_bf = w_out[l].astype(BF16)
        wg_bf = w_gate[l].astype(BF16)
        wu_bf = w_up[l].astype(BF16)
        wd_bf = w_down[l].astype(BF16)
        wgate_bf = jnp.concatenate([gate_a_w[l], gate_x_w[l]], axis=-1).astype(BF16)
        last = l == depth - 1

        zr, q, k, vt, kp_all, vp_all, lx, gz = in_proj(xp, attn_norm_g[l], w_in_bf, TM_PROJ, True, BF16, l, depth,
                                                       () if l == 0 else (kp_all, vp_all))
        ro, r1 = retention(zr, pos_p, ret0, ret_gn_g[l], bp, seq, RET_CHUNK, RET_CHUNK, BF16)
        do = attn_prompt(q, k, vt, lam, diff_ln_g[l], seq, ATT_TQ, ATT_TK, ATT_HEADS, post)
        lo, c1, h1 = conv_lru(lx, conv0, h00, conv_w[l], conv_b[l], wgate_bf, gate_a_b[l],
                              gate_x_b[l], lru_lambda[l], bp, seq, LRU_CHUNK, BF16)
        xp = merge(xp, ro, do, lo, gz, wb_bf, wo_bf, TM_DENSE)
        xp = ffn(xp, ffn_norm_g[l], wg_bf, wu_bf, wd_bf, final_norm_g, TM_DENSE, last)
        outs["rp"].append(r1)
        outs["cp"].append(c1)
        outs["hp"].append(h1.reshape(bp, LRU_W))

        zr, q, k, v, ks_all, vs_all, lx, gz = in_proj(xs, attn_norm_g[l], w_in_bf, tm_s, False, F32, l, depth,
                                                      () if l == 0 else (ks_all, vs_all))
        ro, r2 = retention(zr, pos_s, state_ret[l], ret_gn_g[l], bs, n_new, n_new, LANES, F32)
        wq = _sample_query_blocks(q, n_new)
        do = attn_sample(wq, _new_token_rows(k, n_new), _new_token_rows(v, n_new), lam, diff_ln_g[l],
                         cache_k, cache_v, page_table, l, PAGES_PER_STEP, post)
        do = do.reshape(bs * n_new, DIFF_W)
        lo, c2, h2 = conv_lru(lx, state_conv[l], state_lru[l], conv_w[l], conv_b[l], wgate_bf,
                              gate_a_b[l], gate_x_b[l], lru_lambda[l], bs, n_new, n_new, F32)
        xs = merge(xs, ro, do, lo, gz, wb_bf, wo_bf, tm_s)
        xs = ffn(xs, ffn_norm_g[l], wg_bf, wu_bf, wd_bf, final_norm_g, tm_s, last)
        outs["rs"].append(r2)
        outs["cs"].append(c2)
        outs["hs"].append(h2.reshape(bs, LRU_W))

    st = lambda k: jnp.stack(outs[k])
    return (xp.reshape(bp, seq, D_MODEL), xs.reshape(bs, n_new, D_MODEL),
            heads(kp_all, bp, seq), heads(vp_all, bp, seq), st("rp"), st("cp"), st("hp"),
            heads(ks_all, bs, n_new), heads(vs_all, bs, n_new), st("rs"), st("cs"), st("hs"))
```

```python
import functools
import math

import jax
import jax.numpy as jnp
import numpy as np
from jax import lax
from jax.experimental import pallas as pl
from jax.experimental.pallas import tpu as pltpu

F32 = jnp.float32
BF16 = jnp.bfloat16

D_MODEL = 1024
PAGE_SIZE = 128
RET_HEADS = 4
RET_QK_DIM = 128
RET_V_DIM = 256
RET_QK_W = RET_HEADS * RET_QK_DIM
RET_V_W = RET_HEADS * RET_V_DIM
ROPE_BASE = 10000.0
DIFF_HEADS = 8
DIFF_HEAD_DIM = 64
DIFF_V_DIM = 2 * DIFF_HEAD_DIM
DIFF_W = DIFF_HEADS * DIFF_V_DIM
LRU_W = D_MODEL
LRU_BLOCKS = 4
LRU_BLOCK_W = LRU_W // LRU_BLOCKS
CONV_W = 4
LRU_C = 8.0
N_BRANCH = 3
D_FF = 2816
IN_W = 10240
EPS = 1e-6
LOG2E = 1.4426950408889634

COL_DQ, COL_DK, COL_DV, COL_LX, COL_GZ = 3, 4, 5, 6, 7

LANES = 128
SUBLANES = 8
BF16_ROWS = 16
VMEM_LIMIT = 56 * 1024 * 1024
NEG_BIG = -1e30

TM_PROJ = 256
TM_DENSE = 512
RET_CHUNK = 256
LRU_CHUNK = 256
ATT_TQ = 512
ATT_TK = 1024
ATT_HEADS = 2
PAGES_PER_STEP = 16
PAGE_GROUP = 4


def _cparams(sem):
    return pltpu.CompilerParams(dimension_semantics=sem, vmem_limit_bytes=VMEM_LIMIT)


def _resident(shape, index_map):
    return pl.BlockSpec(shape, index_map, pipeline_mode=pl.Buffered(1))


def _in_proj_kernel(x_ref, g_ref, w_ref, *rest, v_transposed, n_alias):
    zr_ref, q_ref, k_ref, v_ref, kf_ref, vf_ref, lx_ref, gz_ref = rest[n_alias:]
    x = x_ref[...]
    tm = x.shape[0]
    ms = jnp.mean(x * x, axis=-1, keepdims=True)
    h = (x * lax.rsqrt(ms + EPS) * g_ref[...]).astype(BF16)

    def group(j):
        return jnp.dot(h, w_ref[:, j * 1024:(j + 1) * 1024], preferred_element_type=F32)

    def store_heads(ref, z):
        for hd in range(DIFF_HEADS):
            ref[pl.ds(hd, tm, stride=DIFF_HEADS), :] = z[:, hd * DIFF_V_DIM:(hd + 1) * DIFF_V_DIM]

    for j in range(3):
        zr_ref[:, j * 1024:(j + 1) * 1024] = group(j).astype(zr_ref.dtype)
    q_ref[...] = (group(COL_DQ) * (LOG2E * DIFF_HEAD_DIM ** -0.5)).astype(BF16)
    zk = group(COL_DK)
    store_heads(kf_ref, zk)
    k_ref[...] = zk.astype(BF16)
    zv = group(COL_DV)
    store_heads(vf_ref, zv)
    if v_transposed:
        v_ref[0] = zv.T.astype(BF16)
    else:
        v_ref[...] = zv.astype(BF16)
    lx_ref[...] = group(COL_LX)
    for j in range(3):
        gz_ref[:, j * 1024:(j + 1) * 1024] = group(COL_GZ + j).astype(BF16)


def in_proj(x, g, w_bf, tm, v_transposed, zr_dtype, layer, depth, kv_all):
    m = x.shape[0]
    row = lambda i: (i, 0)
    wide = lambda w, dt: (pl.BlockSpec((tm, w), row), jax.ShapeDtypeStruct((m, w), dt))
    if v_transposed:
        v_out = (pl.BlockSpec((1, DIFF_W, tm), lambda i: (i, 0, 0)),
                 jax.ShapeDtypeStruct((m // tm, DIFF_W, tm), BF16))
    else:
        v_out = wide(DIFF_W, BF16)
    slab = (pl.BlockSpec((None, tm * DIFF_HEADS, DIFF_V_DIM), lambda i: (layer, i, 0)),
            jax.ShapeDtypeStruct((depth, m * DIFF_HEADS, DIFF_V_DIM), F32))
    outs = [wide(3 * 1024, zr_dtype), wide(DIFF_W, BF16), wide(DIFF_W, BF16), v_out,
            slab, slab, wide(LRU_W, F32), wide(3 * 1024, BF16)]
    n_alias = len(kv_all)
    return pl.pallas_call(
        functools.partial(_in_proj_kernel, v_transposed=v_transposed, n_alias=n_alias),
        grid=(m // tm,),
        in_specs=[
            pl.BlockSpec((tm, D_MODEL), row),
            pl.BlockSpec((1, D_MODEL), lambda i: (0, 0)),
            _resident((D_MODEL, IN_W), lambda i: (0, 0)),
        ] + [pl.BlockSpec(memory_space=pl.ANY)] * n_alias,
        out_specs=[o[0] for o in outs],
        out_shape=[o[1] for o in outs],
        input_output_aliases={3 + a: 4 + a for a in range(n_alias)},
        compiler_params=_cparams(("arbitrary",)),
        name="in_proj",
    )(x, g.reshape(1, D_MODEL), w_bf, *kv_all)


def _retention_tables(pos, c, cp):
    half = RET_QK_DIM // 2
    inv = ROPE_BASE ** (-jnp.arange(half, dtype=F32) / half)
    ang = pos.astype(F32)[:, None] * inv[None, :]
    cos, sin = jnp.cos(ang), jnp.sin(ang)
    cos2 = jnp.concatenate([cos, cos], axis=-1)
    sin2 = jnp.concatenate([-sin, sin], axis=-1)
    lg = np.log1p(-(2.0 ** (-5.0 - np.arange(RET_HEADS, dtype=np.float64))))
    idx = np.arange(cp, dtype=np.float64)
    rel = idx[:, None] - idx[None, :]
    ok = (rel >= 0) & (idx[:, None] < c) & (idx[None, :] < c)
    dmask = np.where(ok[None], np.exp(lg[:, None, None] * np.maximum(rel, 0.0)[None]), 0.0)
    valid = (idx < c)[:, None]
    rd = np.where(valid, np.exp((idx[:, None] + 1.0) * lg[None, :]), 0.0)
    wd = np.where(valid, np.exp((c - 1.0 - idx)[:, None] * lg[None, :]), 0.0)
    wd = wd * (RET_QK_DIM ** -0.5)
    rd = np.repeat(rd, RET_QK_DIM, axis=1)
    wd = np.repeat(wd, RET_QK_DIM, axis=1)
    cd = np.repeat(np.exp(c * lg), RET_V_DIM)[None, :]
    return (cos2, sin2, jnp.asarray(dmask, F32), jnp.asarray(rd, F32), jnp.asarray(wd, F32),
            jnp.asarray(cd, F32))


def _rope(x, cos2, sin2):
    return x * cos2 + pltpu.roll(x, RET_QK_DIM // 2, axis=1) * sin2


def _retention_kernel(rq_ref, rk_ref, rv_ref, rg_ref, cos_ref, sin_ref, dm_ref, rd_ref, wd_ref,
                      cd_ref, gn_ref, s0_ref, ro_ref, s_out_ref, s_scr, *, c, cp, n_chunks):
    ci = pl.program_id(1)

    @pl.when(ci == 0)
    def _():
        s_scr[...] = s0_ref[0]

    def pad(a):
        if c == cp:
            return a
        return jnp.concatenate([a, jnp.zeros((cp - c, a.shape[1]), a.dtype)], axis=0)

    cos2 = pad(cos_ref[...])
    sin2 = pad(sin_ref[...])
    rq = pad(rq_ref[...].astype(F32))
    rk = pad(rk_ref[...].astype(F32))
    rv = pad(rv_ref[...].astype(F32))
    rd = rd_ref[...]
    wd = wd_ref[...]
    outs = []
    for h in range(RET_HEADS):
        qs = slice(h * RET_QK_DIM, (h + 1) * RET_QK_DIM)
        vs = slice(h * RET_V_DIM, (h + 1) * RET_V_DIM)
        q = _rope(rq[:, qs], cos2, sin2)
        k = _rope(rk[:, qs], cos2, sin2)
        v = rv[:, vs].astype(BF16)
        s_old = s_scr[h]
        sc = lax.dot_general(q.astype(BF16), (k * (RET_QK_DIM ** -0.5)).astype(BF16),
                             (((1,), (1,)), ((), ())), preferred_element_type=F32)
        sc = sc * dm_ref[h]
        o = jnp.dot(sc.astype(BF16), v, preferred_element_type=F32)
        o = o + jnp.dot((q * rd[:, qs]).astype(BF16), s_old.astype(BF16), preferred_element_type=F32)
        kw = (k * wd[:, qs]).T.astype(BF16)
        s_scr[h] = s_old * cd_ref[:, vs] + jnp.dot(kw, v, preferred_element_type=F32)
        mu = jnp.mean(o, axis=-1, keepdims=True)
        d = o - mu
        var = jnp.mean(d * d, axis=-1, keepdims=True)
        outs.append(d * lax.rsqrt(var + EPS))
    o = jnp.concatenate(outs, axis=-1)[:c]
    rg = rg_ref[...].astype(F32)
    ro_ref[...] = (o * gn_ref[...] * (rg * jax.nn.sigmoid(rg))).astype(ro_ref.dtype)

    @pl.when(ci == n_chunks - 1)
    def _():
        s_out_ref[0] = s_scr[...]


def retention(zr, pos, s0, gn_g, batch, length, c, cp, out_dtype):
    n_chunks = length // c
    cos2, sin2, dmask, rd, wd, cd = _retention_tables(pos, c, cp)
    kern = functools.partial(_retention_kernel, c=c, cp=cp, n_chunks=n_chunks)
    row = lambda b, i: b * n_chunks + i
    const2 = lambda b, i: (0, 0)
    return pl.pallas_call(
        kern,
        grid=(batch, n_chunks),
        in_specs=[
            pl.BlockSpec((c, RET_QK_W), lambda b, i: (row(b, i), 0)),
            pl.BlockSpec((c, RET_QK_W), lambda b, i: (row(b, i), 1)),
            pl.BlockSpec((c, RET_V_W), lambda b, i: (row(b, i), 1)),
            pl.BlockSpec((c, RET_V_W), lambda b, i: (row(b, i), 2)),
            pl.BlockSpec((c, RET_QK_DIM), lambda b, i: (i, 0)),
            pl.BlockSpec((c, RET_QK_DIM), lambda b, i: (i, 0)),
            pl.BlockSpec((RET_HEADS, cp, cp), lambda b, i: (0, 0, 0)),
            pl.BlockSpec((cp, RET_QK_W), const2),
            pl.BlockSpec((cp, RET_QK_W), const2),
            pl.BlockSpec((1, RET_V_W), const2),
            pl.BlockSpec((1, RET_V_W), const2),
            pl.BlockSpec((1, RET_HEADS, RET_QK_DIM, RET_V_DIM), lambda b, i: (b, 0, 0, 0)),
        ],
        out_specs=[
            pl.BlockSpec((c, RET_V_W), lambda b, i: (row(b, i), 0)),
            pl.BlockSpec((1, RET_HEADS, RET_QK_DIM, RET_V_DIM), lambda b, i: (b, 0, 0, 0)),
        ],
        out_shape=[
            jax.ShapeDtypeStruct((batch * length, RET_V_W), out_dtype),
            jax.ShapeDtypeStruct((batch, RET_HEADS, RET_QK_DIM, RET_V_DIM), F32),
        ],
        scratch_shapes=[pltpu.VMEM((RET_HEADS, RET_QK_DIM, RET_V_DIM), F32)],
        compiler_params=_cparams(("arbitrary", "arbitrary")),
        name="retention",
    )(zr, zr, zr, zr, cos2, sin2, dmask, rd, wd, cd, gn_g.reshape(1, RET_V_W), s0)


def _lru_kernel(lx_ref, cbuf_ref, h0_ref, cw_ref, cb_ref, wg_ref, ba_ref, bx_ref, lam_ref,
                lo_ref, cnew_ref, hlast_ref, xe_scr, h_scr, a_scr, b_scr, *, t, n_chunks):
    ci = pl.program_id(1)
    keep = CONV_W - 1
    base = SUBLANES - keep

    @pl.when(ci == 0)
    def _():
        xe_scr[base:SUBLANES, :] = cbuf_ref[0]
        h_scr[...] = h0_ref[0]

    xe_scr[SUBLANES:SUBLANES + t, :] = lx_ref[...]
    xc = cb_ref[...] + xe_scr[base:base + t, :] * cw_ref[0:1, :]
    for j in range(1, CONV_W):
        xc = xc + xe_scr[base + j:base + j + t, :] * cw_ref[j:j + 1, :]
    tail = xe_scr[t + base:t + SUBLANES, :]
    xe_scr[base:SUBLANES, :] = tail

    sp = jax.nn.softplus(-lam_ref[...])
    for n in range(LRU_BLOCKS):
        ws = slice(n * LRU_BLOCK_W, (n + 1) * LRU_BLOCK_W)
        xb = xc[:, ws]
        gates = jnp.dot(xb.astype(BF16), wg_ref[n], preferred_element_type=F32)
        r = jax.nn.sigmoid(gates[:, :LRU_BLOCK_W] + ba_ref[:, ws])
        ig = jax.nn.sigmoid(gates[:, LRU_BLOCK_W:] + bx_ref[:, ws])
        log_a = (-LRU_C) * r * sp[:, ws]
        a = jnp.exp(log_a)
        a_scr[:, ws] = a
        b_scr[:, ws] = jnp.sqrt(1.0 - a * a) * (ig * xb)

    def body(i, h):
        r0 = pl.multiple_of(i * SUBLANES, SUBLANES)
        a8 = a_scr[pl.ds(r0, SUBLANES), :]
        b8 = b_scr[pl.ds(r0, SUBLANES), :]
        rows = []
        for s in range(SUBLANES):
            h = a8[s:s + 1, :] * h + b8[s:s + 1, :]
            rows.append(h)
        lo_ref[pl.ds(r0, SUBLANES), :] = jnp.concatenate(rows, axis=0).astype(lo_ref.dtype)
        return h

    h = lax.fori_loop(0, t // SUBLANES, body, h_scr[...])
    h_scr[...] = h

    @pl.when(ci == n_chunks - 1)
    def _():
        cnew_ref[0] = tail
        hlast_ref[0] = h


def conv_lru(lx, cbuf, h0, cw, cb, wg_bf, ba, bx, lam, batch, length, t, out_dtype):
    n_chunks = length // t
    kern = functools.partial(_lru_kernel, t=t, n_chunks=n_chunks)
    c2 = lambda b, i: (0, 0)
    return pl.pallas_call(
        kern,
        grid=(batch, n_chunks),
        in_specs=[
            pl.BlockSpec((t, LRU_W), lambda b, i: (b * n_chunks + i, 0)),
            pl.BlockSpec((1, CONV_W - 1, LRU_W), lambda b, i: (b, 0, 0)),
            pl.BlockSpec((1, 1, LRU_W), lambda b, i: (b, 0, 0)),
            pl.BlockSpec((CONV_W, LRU_W), c2),
            pl.BlockSpec((1, LRU_W), c2),
            pl.BlockSpec((LRU_BLOCKS, LRU_BLOCK_W, 2 * LRU_BLOCK_W), lambda b, i: (0, 0, 0)),
            pl.BlockSpec((1, LRU_W), c2),
            pl.BlockSpec((1, LRU_W), c2),
            pl.BlockSpec((1, LRU_W), c2),
        ],
        out_specs=[
            pl.BlockSpec((t, LRU_W), lambda b, i: (b * n_chunks + i, 0)),
            pl.BlockSpec((1, CONV_W - 1, LRU_W), lambda b, i: (b, 0, 0)),
            pl.BlockSpec((1, 1, LRU_W), lambda b, i: (b, 0, 0)),
        ],
        out_shape=[
            jax.ShapeDtypeStruct((batch * length, LRU_W), out_dtype),
            jax.ShapeDtypeStruct((batch, CONV_W - 1, LRU_W), F32),
            jax.ShapeDtypeStruct((batch, 1, LRU_W), F32),
        ],
        scratch_shapes=[
            pltpu.VMEM((t + SUBLANES, LRU_W), F32),
            pltpu.VMEM((1, LRU_W), F32),
            pltpu.VMEM((t, LRU_W), F32),
            pltpu.VMEM((t, LRU_W), F32),
        ],
        compiler_params=_cparams(("arbitrary", "arbitrary")),
        name="conv_lru",
    )(lx, cbuf, h0.reshape(batch, 1, LRU_W), cw, cb.reshape(1, LRU_W), wg_bf,
      ba.reshape(1, LRU_W), bx.reshape(1, LRU_W), lam.reshape(1, LRU_W))


def _attn_prompt_kernel(lam_ref, q_ref, k_ref, vt_ref, g_ref, bias_ref, o_ref, *, tq, tk, vt_tile,
                        hpb, post_scale):
    i = pl.program_id(1)
    half = DIFF_HEAD_DIM
    lane = lax.broadcasted_iota(jnp.int32, (tq, 2 * half), 1)
    streams = []
    for r in range(hpb):
        q = q_ref[:, r * DIFF_V_DIM:(r + 1) * DIFF_V_DIM]
        zero = jnp.zeros_like(q)
        streams.append((r, jnp.where(lane < half, q, zero)))
        streams.append((r, jnp.where(lane >= half, q, zero)))

    def update(t0, nk, carry, bias):
        row0 = pl.multiple_of(t0 * tq, tq)
        v0 = t0 * (tq // vt_tile)
        ones = jnp.ones((BF16_ROWS, nk), BF16)
        sts = []
        for r, qc in streams:
            k = k_ref[pl.ds(row0, nk), r * DIFF_V_DIM:(r + 1) * DIFF_V_DIM]
            st = lax.dot_general(k, qc, (((1,), (1,)), ((), ())), preferred_element_type=F32)
            sts.append(st if bias is None else st + bias)
        out = []
        for (r, _), st, (m, acc) in zip(streams, sts, carry):
            m_new = jnp.maximum(m, jnp.max(st, axis=0, keepdims=True))
            alpha = jnp.exp2(m - m_new)
            p = jnp.exp2(st - m_new).astype(BF16)
            va = jnp.concatenate([vt_ref[v0 + t, r * DIFF_V_DIM:(r + 1) * DIFF_V_DIM, :]
                                  for t in range(nk // vt_tile)], axis=1)
            va = jnp.concatenate([va, ones], axis=0)
            out.append((m_new, alpha * acc + jnp.dot(va, p, preferred_element_type=F32)))
        return tuple(out)

    init = tuple((jnp.full((1, tq), NEG_BIG, F32), jnp.zeros((DIFF_V_DIM + BF16_ROWS, tq), F32))
                 for _ in streams)
    per = tk // tq
    carry = lax.fori_loop(0, i // per, lambda j, c: update(j * per, tk, c, None), init)
    carry = lax.cond(i % per == 1, lambda c: update(i - 1, tq, c, None), lambda c: c, carry)
    carry = update(i, tq, carry, bias_ref[...])
    for r in range(hpb):
        a1, a2 = (acc[:DIFF_V_DIM] / acc[DIFF_V_DIM:DIFF_V_DIM + 1] for _, acc in carry[2 * r:2 * r + 2])
        ot = a1 - lam_ref[0] * a2
        ms = jnp.mean(ot * ot, axis=0, keepdims=True)
        ot = ot * lax.rsqrt(ms + EPS)
        o_ref[:, r * DIFF_V_DIM:(r + 1) * DIFF_V_DIM] = (ot.T * g_ref[...] * post_scale).astype(o_ref.dtype)


def attn_prompt(q, k, vt, lam, g, seq, tq, tk, hpb, post_scale):
    assert tk == 2 * tq, "full key tiles are two query tiles long"
    vt_tile = vt.shape[2]
    kern = functools.partial(_attn_prompt_kernel, tq=tq, tk=tk, vt_tile=vt_tile, hpb=hpb,
                             post_scale=post_scale)
    kk = np.arange(tq)[:, None]
    qq = np.arange(tq)[None, :]
    bias = jnp.asarray(np.where(kk <= qq, 0.0, NEG_BIG).astype(np.float32))
    w = hpb * DIFF_V_DIM
    return pl.pallas_call(
        kern,
        grid=(DIFF_HEADS // hpb, seq // tq),
        in_specs=[
            pl.BlockSpec(memory_space=pltpu.SMEM),
            pl.BlockSpec((tq, w), lambda h, i: (i, h)),
            pl.BlockSpec((seq, w), lambda h, i: (0, h)),
            pl.BlockSpec((seq // vt_tile, w, vt_tile), lambda h, i: (0, h, 0)),
            pl.BlockSpec((1, DIFF_V_DIM), lambda h, i: (0, 0)),
            pl.BlockSpec((tq, tq), lambda h, i: (0, 0)),
        ],
        out_specs=pl.BlockSpec((tq, w), lambda h, i: (i, h)),
        out_shape=jax.ShapeDtypeStruct((seq, DIFF_W), BF16),
        compiler_params=_cparams(("arbitrary", "arbitrary")),
        name="attn_prompt",
    )(lam.reshape(1), q, k, vt, g.reshape(1, DIFF_V_DIM), bias)


def _attn_sample_kernel(pt_ref, lam_ref, wq_ref, kn_ref, vn_ref, g_ref, bias_ref, nbias_ref, *rest,
                        pages, group, n_steps, n_new, post_scale):
    k_refs = rest[:pages]
    v_refs = rest[pages:2 * pages]
    o_ref = rest[2 * pages]
    m_scr, l_scr, acc_scr = rest[2 * pages + 1:]
    step = pl.program_id(1)
    cols = PAGE_SIZE * DIFF_HEADS

    @pl.when(step == 0)
    def _():
        m_scr[...] = jnp.full(m_scr.shape, NEG_BIG, F32)
        l_scr[...] = jnp.zeros(l_scr.shape, F32)
        acc_scr[...] = jnp.zeros(acc_scr.shape, F32)

    wq = wq_ref[0]

    def online(s_list, v_list):
        m_old = m_scr[...]
        m_new = m_old
        for s in s_list:
            m_new = jnp.maximum(m_new, jnp.max(s, axis=-1, keepdims=True))
        alpha = jnp.exp2(m_old - m_new)
        l = alpha * l_scr[...]
        acc = alpha * acc_scr[...]
        for s, v in zip(s_list, v_list):
            p = jnp.exp2(s - m_new)
            l = l + jnp.sum(p, axis=-1, keepdims=True)
            acc = acc + jnp.dot(p.astype(BF16), v, preferred_element_type=F32)
        m_scr[...] = m_new
        l_scr[...] = l
        acc_scr[...] = acc

    bias = bias_ref[...]

    def scores(g):
        k2 = k_refs[g][...].reshape(cols, DIFF_V_DIM).astype(BF16)
        s = lax.dot_general(wq, k2, (((1,), (1,)), ((), ())), preferred_element_type=F32)
        return s + bias

    def values(g):
        return v_refs[g][...].reshape(cols, DIFF_V_DIM).astype(BF16)

    groups = [list(range(g0, g0 + group)) for g0 in range(0, pages, group)]
    s_next = [scores(g) for g in groups[0]]
    for gi, grp in enumerate(groups):
        s_cur = s_next
        if gi + 1 < len(groups):
            s_next = [scores(g) for g in groups[gi + 1]]
        online(s_cur, [values(g) for g in grp])

    @pl.when(step == n_steps - 1)
    def _():
        s = lax.dot_general(wq, kn_ref[0], (((1,), (1,)), ((), ())), preferred_element_type=F32)
        online([s + nbias_ref[...]], [vn_ref[0]])
        a = acc_scr[...] / l_scr[...]
        lam = lam_ref[0]
        outs = []
        for h in range(DIFF_HEADS):
            r0 = h * 2 * n_new
            o = a[r0:r0 + n_new] - lam * a[r0 + n_new:r0 + 2 * n_new]
            ms = jnp.mean(o * o, axis=-1, keepdims=True)
            outs.append(o * lax.rsqrt(ms + EPS) * g_ref[...] * post_scale)
        o_ref[0] = jnp.concatenate(outs, axis=-1)


def _sample_biases(n_new):
    rows = 2 * DIFF_HEADS * n_new
    r = np.arange(rows)[:, None]
    c = np.arange(PAGE_SIZE * DIFF_HEADS)[None, :]
    bias = np.where((r // (2 * n_new)) == (c % DIFF_HEADS), 0.0, NEG_BIG)
    c = np.arange(LANES)[None, :]
    ok = ((r // (2 * n_new)) == (c % DIFF_HEADS)) & ((c // DIFF_HEADS) <= (r % n_new)) \
        & (c < n_new * DIFF_HEADS)
    return jnp.asarray(bias, F32), jnp.asarray(np.where(ok, 0.0, NEG_BIG), F32)


def attn_sample(wq, k_new, v_new, lam, g, cache_k, cache_v, page_table, layer, pages, post_scale):
    batch, n_pages = page_table.shape
    n_new = wq.shape[1] // (2 * DIFF_HEADS)
    n_steps = n_pages // pages
    kern = functools.partial(_attn_sample_kernel, pages=pages, group=PAGE_GROUP, n_steps=n_steps,
                             n_new=n_new, post_scale=post_scale)
    bias, nbias = _sample_biases(n_new)

    def page_spec(gi):
        return pl.BlockSpec((None, None, PAGE_SIZE, DIFF_HEADS, DIFF_V_DIM),
                            lambda b, p, pt: (layer, pt[b, p * pages + gi], 0, 0, 0))

    per_b = lambda b, p, pt: (b, 0, 0)
    c2 = lambda b, p, pt: (0, 0)
    rows = wq.shape[1]
    grid_spec = pltpu.PrefetchScalarGridSpec(
        num_scalar_prefetch=1,
        grid=(batch, n_steps),
        in_specs=[
            pl.BlockSpec(memory_space=pltpu.SMEM),
            pl.BlockSpec((1, rows, DIFF_V_DIM), per_b),
            pl.BlockSpec((1,) + k_new.shape[1:], per_b),
            pl.BlockSpec((1,) + v_new.shape[1:], per_b),
            pl.BlockSpec((1, DIFF_V_DIM), c2),
            pl.BlockSpec(bias.shape, c2),
            pl.BlockSpec(nbias.shape, c2),
        ] + [page_spec(gi) for gi in range(pages)] * 2,
        out_specs=pl.BlockSpec((1, n_new, DIFF_W), per_b),
        scratch_shapes=[
            pltpu.VMEM((rows, 1), F32),
            pltpu.VMEM((rows, 1), F32),
            pltpu.VMEM((rows, DIFF_V_DIM), F32),
        ],
    )
    return pl.pallas_call(
        kern,
        grid_spec=grid_spec,
        out_shape=jax.ShapeDtypeStruct((batch, n_new, DIFF_W), F32),
        compiler_params=_cparams(("arbitrary", "arbitrary")),
        name="attn_sample",
    )(page_table, lam.reshape(1), wq, k_new, v_new, g.reshape(1, DIFF_V_DIM), bias, nbias,
      *([cache_k] * pages), *([cache_v] * pages))


def _merge_kernel(x_ref, ro_ref, do_ref, lo_ref, gz_ref, wb_ref, wo_ref, y_ref):
    mix = None
    for n, br in enumerate((ro_ref, do_ref, lo_ref)):
        proj = jnp.dot(br[...].astype(BF16), wb_ref[n], preferred_element_type=F32)
        gz = gz_ref[:, n * D_MODEL:(n + 1) * D_MODEL].astype(F32)
        term = jax.nn.sigmoid(gz) * proj
        mix = term if mix is None else mix + term
    y_ref[...] = x_ref[...] + jnp.dot(mix.astype(BF16), wo_ref[...], preferred_element_type=F32)


def merge(x, ro, do, lo, gz, wb_bf, wo_bf, tm):
    m = x.shape[0]
    row = lambda i: (i, 0)
    return pl.pallas_call(
        _merge_kernel,
        grid=(m // tm,),
        in_specs=[
            pl.BlockSpec((tm, D_MODEL), row),
            pl.BlockSpec((tm, D_MODEL), row),
            pl.BlockSpec((tm, D_MODEL), row),
            pl.BlockSpec((tm, D_MODEL), row),
            pl.BlockSpec((tm, N_BRANCH * D_MODEL), row),
            _resident((N_BRANCH, D_MODEL, D_MODEL), lambda i: (0, 0, 0)),
            _resident((D_MODEL, D_MODEL), lambda i: (0, 0)),
        ],
        out_specs=pl.BlockSpec((tm, D_MODEL), row),
        out_shape=jax.ShapeDtypeStruct((m, D_MODEL), F32),
        compiler_params=_cparams(("arbitrary",)),
        name="merge",
    )(x, ro, do, lo, gz, wb_bf, wo_bf)


FF_CHUNK = D_FF // 2


def _ffn_kernel(x_ref, g_ref, wg_ref, wu_ref, wd_ref, fg_ref, y_ref, *, final_norm):
    x = x_ref[...]
    ms = jnp.mean(x * x, axis=-1, keepdims=True)
    h = (x * lax.rsqrt(ms + EPS) * g_ref[...]).astype(BF16)
    y = x
    for c in range(D_FF // FF_CHUNK):
        cs = slice(c * FF_CHUNK, (c + 1) * FF_CHUNK)
        gate = jnp.dot(h, wg_ref[:, cs], preferred_element_type=F32)
        up = jnp.dot(h, wu_ref[:, cs], preferred_element_type=F32)
        act = (gate * jax.nn.sigmoid(gate) * up).astype(BF16)
        y = y + jnp.dot(act, wd_ref[cs, :], preferred_element_type=F32)
    if final_norm:
        ms = jnp.mean(y * y, axis=-1, keepdims=True)
        y = y * lax.rsqrt(ms + EPS) * fg_ref[...]
    y_ref[...] = y


def ffn(x, g, wg_bf, wu_bf, wd_bf, final_g, tm, final_norm):
    m = x.shape[0]
    c2 = lambda i: (0, 0)
    return pl.pallas_call(
        functools.partial(_ffn_kernel, final_norm=final_norm),
        grid=(m // tm,),
        in_specs=[
            pl.BlockSpec((tm, D_MODEL), lambda i: (i, 0)),
            pl.BlockSpec((1, D_MODEL), c2),
            _resident((D_MODEL, D_FF), c2),
            _resident((D_MODEL, D_FF), c2),
            _resident((D_FF, D_MODEL), c2),
            pl.BlockSpec((1, D_MODEL), c2),
        ],
        out_specs=pl.BlockSpec((tm, D_MODEL), lambda i: (i, 0)),
        out_shape=jax.ShapeDtypeStruct((m, D_MODEL), F32),
        compiler_params=_cparams(("arbitrary",)),
        name="ffn",
    )(x, g.reshape(1, D_MODEL), wg_bf, wu_bf, wd_bf, final_g.reshape(1, D_MODEL))


def _sample_query_blocks(dq, n_new):
    b = dq.shape[0] // n_new
    q = dq.reshape(b, n_new, DIFF_HEADS, 2, DIFF_HEAD_DIM).transpose(0, 2, 3, 1, 4)
    eye = jnp.eye(2, dtype=q.dtype)
    blk = q[:, :, :, :, None, :] * eye[None, None, :, None, :, None]
    return blk.reshape(b, DIFF_HEADS * 2 * n_new, 2 * DIFF_HEAD_DIM)


def _new_token_rows(x, n_new):
    b = x.shape[0] // n_new
    r = x.reshape(b, n_new * DIFF_HEADS, DIFF_V_DIM)
    return jnp.pad(r, ((0, 0), (0, LANES - n_new * DIFF_HEADS), (0, 0)))


def kernel(x_prompt, x_sample, cache_k, cache_v, page_table, state_ret, state_conv, state_lru,
           attn_norm_g, w_in, ret_gn_g, diff_lambda, diff_ln_g, conv_w, conv_b,
           gate_a_w, gate_a_b, gate_x_w, gate_x_b, lru_lambda, w_branch, w_out,
           ffn_norm_g, w_gate, w_up, w_down, final_norm_g):
    bp, seq, _ = x_prompt.shape
    bs, n_new, _ = x_sample.shape
    depth = w_in.shape[0]
    past = page_table.shape[1] * PAGE_SIZE
    pos_p = jnp.arange(seq)
    pos_s = past + jnp.arange(n_new)

    xp = x_prompt.reshape(bp * seq, D_MODEL)
    xs = x_sample.reshape(bs * n_new, D_MODEL)
    ret0 = jnp.zeros((bp, RET_HEADS, RET_QK_DIM, RET_V_DIM), F32)
    conv0 = jnp.zeros((bp, CONV_W - 1, LRU_W), F32)
    h00 = jnp.zeros((bp, LRU_W), F32)

    tm_s = bs * n_new
    heads = lambda a, b, n: a.reshape(depth, b, n, DIFF_HEADS, DIFF_V_DIM)
    outs = {k: [] for k in ("rp", "cp", "hp", "rs", "cs", "hs")}
    for l in range(depth):
        lam_init = 0.8 - 0.6 * math.exp(-0.3 * l)
        post = 1.0 - lam_init
        lq1, lk1, lq2, lk2 = diff_lambda[l].astype(F32)
        lam = jnp.exp(jnp.sum(lq1 * lk1)) - jnp.exp(jnp.sum(lq2 * lk2)) + lam_init
        w_in_bf = w_in[l].astype(BF16)
        wb_bf = w_branch[l].astype(BF16)
        wo_bf = w_out[l].astype(BF16)
        wg_bf = w_gate[l].astype(BF16)
        wu_bf = w_up[l].astype(BF16)
        wd_bf = w_down[l].astype(BF16)
        wgate_bf = jnp.concatenate([gate_a_w[l], gate_x_w[l]], axis=-1).astype(BF16)
        last = l == depth - 1

        zr, q, k, vt, kp_all, vp_all, lx, gz = in_proj(xp, attn_norm_g[l], w_in_bf, TM_PROJ, True, BF16, l, depth,
                                                       () if l == 0 else (kp_all, vp_all))
        ro, r1 = retention(zr, pos_p, ret0, ret_gn_g[l], bp, seq, RET_CHUNK, RET_CHUNK, BF16)
        do = attn_prompt(q, k, vt, lam, diff_ln_g[l], seq, ATT_TQ, ATT_TK, ATT_HEADS, post)
        lo, c1, h1 = conv_lru(lx, conv0, h00, conv_w[l], conv_b[l], wgate_bf, gate_a_b[l],
                              gate_x_b[l], lru_lambda[l], bp, seq, LRU_CHUNK, BF16)
        xp = merge(xp, ro, do, lo, gz, wb_bf, wo_bf, TM_DENSE)
        xp = ffn(xp, ffn_norm_g[l], wg_bf, wu_bf, wd_bf, final_norm_g, TM_DENSE, last)
        outs["rp"].append(r1)
        outs["cp"].append(c1)
        outs["hp"].append(h1.reshape(bp, LRU_W))

        zr, q, k, v, ks_all, vs_all, lx, gz = in_proj(xs, attn_norm_g[l], w_in_bf, tm_s, False, F32, l, depth,
                                                      () if l == 0 else (ks_all, vs_all))
        ro, r2 = retention(zr, pos_s, state_ret[l], ret_gn_g[l], bs, n_new, n_new, LANES, F32)
        wq = _sample_query_blocks(q, n_new)
        do = attn_sample(wq, _new_token_rows(k, n_new), _new_token_rows(v, n_new), lam, diff_ln_g[l],
                         cache_k, cache_v, page_table, l, PAGES_PER_STEP, post)
        do = do.reshape(bs * n_new, DIFF_W)
        lo, c2, h2 = conv_lru(lx, state_conv[l], state_lru[l], conv_w[l], conv_b[l], wgate_bf,
                              gate_a_b[l], gate_x_b[l], lru_lambda[l], bs, n_new, n_new, F32)
        xs = merge(xs, ro, do, lo, gz, wb_bf, wo_bf, tm_s)
        xs = ffn(xs, ffn_norm_g[l], wg_bf, wu_bf, wd_bf, final_norm_g, tm_s, last)
        outs["rs"].append(r2)
        outs["cs"].append(c2)
        outs["hs"].append(h2.reshape(bs, LRU_W))

    st = lambda k: jnp.stack(outs[k])
    return (xp.reshape(bp, seq, D_MODEL), xs.reshape(bs, n_new, D_MODEL),
            heads(kp_all, bp, seq), heads(vp_all, bp, seq), st("rp"), st("cp"), st("hp"),
            heads(ks_all, bs, n_new), heads(vs_all, bs, n_new), st("rs"), st("cs"), st("hs"))
```

```python
import functools
import math

import jax
import jax.numpy as jnp
import numpy as np
from jax import lax
from jax.experimental import pallas as pl
from jax.experimental.pallas import tpu as pltpu

F32 = jnp.float32
BF16 = jnp.bfloat16

D_MODEL = 1024
PAGE_SIZE = 128
RET_HEADS = 4
RET_QK_DIM = 128
RET_V_DIM = 256
RET_QK_W = RET_HEADS * RET_QK_DIM
RET_V_W = RET_HEADS * RET_V_DIM
ROPE_BASE = 10000.0
DIFF_HEADS = 8
DIFF_HEAD_DIM = 64
DIFF_V_DIM = 2 * DIFF_HEAD_DIM
DIFF_W = DIFF_HEADS * DIFF_V_DIM
LRU_W = D_MODEL
LRU_BLOCKS = 4
LRU_BLOCK_W = LRU_W // LRU_BLOCKS
CONV_W = 4
LRU_C = 8.0
N_BRANCH = 3
D_FF = 2816
IN_W = 10240
EPS = 1e-6
LOG2E = 1.4426950408889634

GROUP_W = D_MODEL
COL_DQ, COL_DK, COL_DV, COL_LX, COL_GZ = 3, 4, 5, 6, 7

LANES = 128
SUBLANES = 8
BF16_ROWS = 16
VMEM_LIMIT = 56 * 1024 * 1024
NEG_BIG = -1e30

TM_PROJ = 256
TM_DENSE = 512
RET_CHUNK = 256
LRU_CHUNK = 256
ATT_TQ = 512
ATT_TK = 1024
ATT_HEADS = 2
PAGES_PER_STEP = 16
PAGE_GROUP = 4


def _cparams(sem):
    return pltpu.CompilerParams(dimension_semantics=sem, vmem_limit_bytes=VMEM_LIMIT)


def _resident(shape, index_map):
    return pl.BlockSpec(shape, index_map, pipeline_mode=pl.Buffered(1))


def _in_proj_kernel(x_ref, g_ref, w_ref, *rest, v_transposed, n_alias):
    zr_ref, q_ref, k_ref, v_ref, kf_ref, vf_ref, lx_ref, gz_ref = rest[n_alias:]
    x = x_ref[...]
    tm = x.shape[0]
    ms = jnp.mean(x * x, axis=-1, keepdims=True)
    h = (x * lax.rsqrt(ms + EPS) * g_ref[...]).astype(BF16)

    def group(j):
        return jnp.dot(h, w_ref[:, j * GROUP_W:(j + 1) * GROUP_W], preferred_element_type=F32)

    def store_heads(ref, z):
        for hd in range(DIFF_HEADS):
            ref[pl.ds(hd, tm, stride=DIFF_HEADS), :] = z[:, hd * DIFF_V_DIM:(hd + 1) * DIFF_V_DIM]

    for j in range(3):
        zr_ref[:, j * GROUP_W:(j + 1) * GROUP_W] = group(j).astype(zr_ref.dtype)
    q_ref[...] = (group(COL_DQ) * (LOG2E * DIFF_HEAD_DIM ** -0.5)).astype(BF16)
    zk = group(COL_DK)
    store_heads(kf_ref, zk)
    k_ref[...] = zk.astype(BF16)
    zv = group(COL_DV)
    store_heads(vf_ref, zv)
    if v_transposed:
        v_ref[0] = zv.T.astype(BF16)
    else:
        v_ref[...] = zv.astype(BF16)
    lx_ref[...] = group(COL_LX)
    for j in range(3):
        gz_ref[:, j * GROUP_W:(j + 1) * GROUP_W] = group(COL_GZ + j).astype(BF16)


def in_proj(x, g, w_bf, tm, v_transposed, zr_dtype, layer, depth, kv_all):
    m = x.shape[0]
    row = lambda i: (i, 0)
    wide = lambda w, dt: (pl.BlockSpec((tm, w), row), jax.ShapeDtypeStruct((m, w), dt))
    if v_transposed:
        v_out = (pl.BlockSpec((1, DIFF_W, tm), lambda i: (i, 0, 0)),
                 jax.ShapeDtypeStruct((m // tm, DIFF_W, tm), BF16))
    else:
        v_out = wide(DIFF_W, BF16)
    slab = (pl.BlockSpec((None, tm * DIFF_HEADS, DIFF_V_DIM), lambda i: (layer, i, 0)),
            jax.ShapeDtypeStruct((depth, m * DIFF_HEADS, DIFF_V_DIM), F32))
    outs = [wide(3 * GROUP_W, zr_dtype), wide(DIFF_W, BF16), wide(DIFF_W, BF16), v_out,
            slab, slab, wide(LRU_W, F32), wide(N_BRANCH * GROUP_W, BF16)]
    n_alias = len(kv_all)
    return pl.pallas_call(
        functools.partial(_in_proj_kernel, v_transposed=v_transposed, n_alias=n_alias),
        grid=(m // tm,),
        in_specs=[
            pl.BlockSpec((tm, D_MODEL), row),
            pl.BlockSpec((1, D_MODEL), lambda i: (0, 0)),
            _resident((D_MODEL, IN_W), lambda i: (0, 0)),
        ] + [pl.BlockSpec(memory_space=pl.ANY)] * n_alias,
        out_specs=[o[0] for o in outs],
        out_shape=[o[1] for o in outs],
        input_output_aliases={3 + a: 4 + a for a in range(n_alias)},
        compiler_params=_cparams(("arbitrary",)),
        name="in_proj",
    )(x, g.reshape(1, D_MODEL), w_bf, *kv_all)


def _retention_tables(pos, c, cp):
    half = RET_QK_DIM // 2
    inv = ROPE_BASE ** (-jnp.arange(half, dtype=F32) / half)
    ang = pos.astype(F32)[:, None] * inv[None, :]
    cos, sin = jnp.cos(ang), jnp.sin(ang)
    cos2 = jnp.concatenate([cos, cos], axis=-1)
    sin2 = jnp.concatenate([-sin, sin], axis=-1)
    lg = np.log1p(-(2.0 ** (-5.0 - np.arange(RET_HEADS, dtype=np.float64))))
    idx = np.arange(cp, dtype=np.float64)
    rel = idx[:, None] - idx[None, :]
    ok = (rel >= 0) & (idx[:, None] < c) & (idx[None, :] < c)
    dmask = np.where(ok[None], np.exp(lg[:, None, None] * np.maximum(rel, 0.0)[None]), 0.0)
    valid = (idx < c)[:, None]
    rd = np.where(valid, np.exp((idx[:, None] + 1.0) * lg[None, :]), 0.0)
    wd = np.where(valid, np.exp((c - 1.0 - idx)[:, None] * lg[None, :]), 0.0)
    wd = wd * (RET_QK_DIM ** -0.5)
    rd = np.repeat(rd, RET_QK_DIM, axis=1)
    wd = np.repeat(wd, RET_QK_DIM, axis=1)
    cd = np.repeat(np.exp(c * lg), RET_V_DIM)[None, :]
    return (cos2, sin2, jnp.asarray(dmask, F32), jnp.asarray(rd, F32), jnp.asarray(wd, F32),
            jnp.asarray(cd, F32))


def _rope(x, cos2, sin2):
    return x * cos2 + pltpu.roll(x, RET_QK_DIM // 2, axis=1) * sin2


def _retention_kernel(rq_ref, rk_ref, rv_ref, rg_ref, cos_ref, sin_ref, dm_ref, rd_ref, wd_ref,
                      cd_ref, gn_ref, s0_ref, *rest, c, cp, n_chunks):
    ro_ref, s_out_ref, s_scr = rest[-3:]
    ci = pl.program_id(1)

    @pl.when(ci == 0)
    def _():
        s_scr[...] = s0_ref[0]

    def pad(a):
        if c == cp:
            return a
        return jnp.concatenate([a, jnp.zeros((cp - c, a.shape[1]), a.dtype)], axis=0)

    cos2 = pad(cos_ref[...])
    sin2 = pad(sin_ref[...])
    rq = pad(rq_ref[...].astype(F32))
    rk = pad(rk_ref[...].astype(F32))
    rv = pad(rv_ref[...].astype(F32))
    rd = rd_ref[...]
    wd = wd_ref[...]
    outs = []
    for h in range(RET_HEADS):
        qs = slice(h * RET_QK_DIM, (h + 1) * RET_QK_DIM)
        vs = slice(h * RET_V_DIM, (h + 1) * RET_V_DIM)
        q = _rope(rq[:, qs], cos2, sin2)
        k = _rope(rk[:, qs], cos2, sin2)
        v = rv[:, vs].astype(BF16)
        s_old = s_scr[h]
        sc = lax.dot_general(q.astype(BF16), (k * (RET_QK_DIM ** -0.5)).astype(BF16),
                             (((1,), (1,)), ((), ())), preferred_element_type=F32)
        sc = sc * dm_ref[h]
        o = jnp.dot(sc.astype(BF16), v, preferred_element_type=F32)
        o = o + jnp.dot((q * rd[:, qs]).astype(BF16), s_old.astype(BF16), preferred_element_type=F32)
        kw = (k * wd[:, qs]).T.astype(BF16)
        s_scr[h] = s_old * cd_ref[:, vs] + jnp.dot(kw, v, preferred_element_type=F32)
        mu = jnp.mean(o, axis=-1, keepdims=True)
        d = o - mu
        var = jnp.mean(d * d, axis=-1, keepdims=True)
        outs.append(d * lax.rsqrt(var + EPS))
    o = jnp.concatenate(outs, axis=-1)[:c]
    rg = rg_ref[...].astype(F32)
    ro_ref[...] = (o * gn_ref[...] * (rg * jax.nn.sigmoid(rg))).astype(ro_ref.dtype)

    @pl.when(ci == n_chunks - 1)
    def _():
        s_out_ref[0] = s_scr[...]


def retention(zr, pos, s0, gn_g, batch, length, c, cp, out_dtype, layer, depth, s_all):
    n_chunks = length // c
    state_block = (None, 1, RET_HEADS, RET_QK_DIM, RET_V_DIM)
    cos2, sin2, dmask, rd, wd, cd = _retention_tables(pos, c, cp)
    kern = functools.partial(_retention_kernel, c=c, cp=cp, n_chunks=n_chunks)
    row = lambda b, i: b * n_chunks + i
    const2 = lambda b, i: (0, 0)
    return pl.pallas_call(
        kern,
        grid=(batch, n_chunks),
        in_specs=[
            pl.BlockSpec((c, RET_QK_W), lambda b, i: (row(b, i), 0)),
            pl.BlockSpec((c, RET_QK_W), lambda b, i: (row(b, i), 1)),
            pl.BlockSpec((c, RET_V_W), lambda b, i: (row(b, i), 1)),
            pl.BlockSpec((c, RET_V_W), lambda b, i: (row(b, i), 2)),
            pl.BlockSpec((c, RET_QK_DIM), lambda b, i: (i, 0)),
            pl.BlockSpec((c, RET_QK_DIM), lambda b, i: (i, 0)),
            pl.BlockSpec((RET_HEADS, cp, cp), lambda b, i: (0, 0, 0)),
            pl.BlockSpec((cp, RET_QK_W), const2),
            pl.BlockSpec((cp, RET_QK_W), const2),
            pl.BlockSpec((1, RET_V_W), const2),
            pl.BlockSpec((1, RET_V_W), const2),
            pl.BlockSpec((1, RET_HEADS, RET_QK_DIM, RET_V_DIM), lambda b, i: (b, 0, 0, 0)),
        ] + [pl.BlockSpec(memory_space=pl.ANY)] * len(s_all),
        out_specs=[
            pl.BlockSpec((c, RET_V_W), lambda b, i: (row(b, i), 0)),
            pl.BlockSpec(state_block, lambda b, i: (layer, b, 0, 0, 0)),
        ],
        out_shape=[
            jax.ShapeDtypeStruct((batch * length, RET_V_W), out_dtype),
            jax.ShapeDtypeStruct((depth, batch, RET_HEADS, RET_QK_DIM, RET_V_DIM), F32),
        ],
        input_output_aliases={12: 1} if s_all else {},
        scratch_shapes=[pltpu.VMEM((RET_HEADS, RET_QK_DIM, RET_V_DIM), F32)],
        compiler_params=_cparams(("arbitrary", "arbitrary")),
        name="retention",
    )(zr, zr, zr, zr, cos2, sin2, dmask, rd, wd, cd, gn_g.reshape(1, RET_V_W), s0, *s_all)


def _lru_kernel(lx_ref, cbuf_ref, h0_ref, cw_ref, cb_ref, wg_ref, ba_ref, bx_ref, lam_ref,
                lo_ref, cnew_ref, hlast_ref, xe_scr, h_scr, a_scr, b_scr, *, t, n_chunks):
    ci = pl.program_id(1)
    keep = CONV_W - 1
    base = SUBLANES - keep

    @pl.when(ci == 0)
    def _():
        xe_scr[base:SUBLANES, :] = cbuf_ref[0]
        h_scr[...] = h0_ref[0]

    xe_scr[SUBLANES:SUBLANES + t, :] = lx_ref[...]
    xc = cb_ref[...] + xe_scr[base:base + t, :] * cw_ref[0:1, :]
    for j in range(1, CONV_W):
        xc = xc + xe_scr[base + j:base + j + t, :] * cw_ref[j:j + 1, :]
    tail = xe_scr[t + base:t + SUBLANES, :]
    xe_scr[base:SUBLANES, :] = tail

    sp = jax.nn.softplus(-lam_ref[...])
    for n in range(LRU_BLOCKS):
        ws = slice(n * LRU_BLOCK_W, (n + 1) * LRU_BLOCK_W)
        xb = xc[:, ws]
        gates = jnp.dot(xb.astype(BF16), wg_ref[n], preferred_element_type=F32)
        r = jax.nn.sigmoid(gates[:, :LRU_BLOCK_W] + ba_ref[:, ws])
        ig = jax.nn.sigmoid(gates[:, LRU_BLOCK_W:] + bx_ref[:, ws])
        log_a = (-LRU_C) * r * sp[:, ws]
        a = jnp.exp(log_a)
        a_scr[:, ws] = a
        b_scr[:, ws] = jnp.sqrt(1.0 - a * a) * (ig * xb)

    def body(i, h):
        r0 = pl.multiple_of(i * SUBLANES, SUBLANES)
        a8 = a_scr[pl.ds(r0, SUBLANES), :]
        b8 = b_scr[pl.ds(r0, SUBLANES), :]
        rows = []
        for s in range(SUBLANES):
            h = a8[s:s + 1, :] * h + b8[s:s + 1, :]
            rows.append(h)
        lo_ref[pl.ds(r0, SUBLANES), :] = jnp.concatenate(rows, axis=0).astype(lo_ref.dtype)
        return h

    h = lax.fori_loop(0, t // SUBLANES, body, h_scr[...])
    h_scr[...] = h

    @pl.when(ci == n_chunks - 1)
    def _():
        cnew_ref[0] = tail
        hlast_ref[0] = h


def conv_lru(lx, cbuf, h0, cw, cb, wg_bf, ba, bx, lam, batch, length, t, out_dtype):
    n_chunks = length // t
    kern = functools.partial(_lru_kernel, t=t, n_chunks=n_chunks)
    c2 = lambda b, i: (0, 0)
    return pl.pallas_call(
        kern,
        grid=(batch, n_chunks),
        in_specs=[
            pl.BlockSpec((t, LRU_W), lambda b, i: (b * n_chunks + i, 0)),
            pl.BlockSpec((1, CONV_W - 1, LRU_W), lambda b, i: (b, 0, 0)),
            pl.BlockSpec((1, 1, LRU_W), lambda b, i: (b, 0, 0)),
            pl.BlockSpec((CONV_W, LRU_W), c2),
            pl.BlockSpec((1, LRU_W), c2),
            pl.BlockSpec((LRU_BLOCKS, LRU_BLOCK_W, 2 * LRU_BLOCK_W), lambda b, i: (0, 0, 0)),
            pl.BlockSpec((1, LRU_W), c2),
            pl.BlockSpec((1, LRU_W), c2),
            pl.BlockSpec((1, LRU_W), c2),
        ],
        out_specs=[
            pl.BlockSpec((t, LRU_W), lambda b, i: (b * n_chunks + i, 0)),
            pl.BlockSpec((1, CONV_W - 1, LRU_W), lambda b, i: (b, 0, 0)),
            pl.BlockSpec((1, 1, LRU_W), lambda b, i: (b, 0, 0)),
        ],
        out_shape=[
            jax.ShapeDtypeStruct((batch * length, LRU_W), out_dtype),
            jax.ShapeDtypeStruct((batch, CONV_W - 1, LRU_W), F32),
            jax.ShapeDtypeStruct((batch, 1, LRU_W), F32),
        ],
        scratch_shapes=[
            pltpu.VMEM((t + SUBLANES, LRU_W), F32),
            pltpu.VMEM((1, LRU_W), F32),
            pltpu.VMEM((t, LRU_W), F32),
            pltpu.VMEM((t, LRU_W), F32),
        ],
        compiler_params=_cparams(("arbitrary", "arbitrary")),
        name="conv_lru",
    )(lx, cbuf, h0.reshape(batch, 1, LRU_W), cw, cb.reshape(1, LRU_W), wg_bf,
      ba.reshape(1, LRU_W), bx.reshape(1, LRU_W), lam.reshape(1, LRU_W))


def _attn_prompt_kernel(lam_ref, q_ref, k_ref, vt_ref, g_ref, bias_ref, o_ref, *, tq, tk, vt_tile,
                        hpb, post_scale):
    i = pl.program_id(1)
    half = DIFF_HEAD_DIM
    lane = lax.broadcasted_iota(jnp.int32, (tq, 2 * half), 1)
    streams = []
    for r in range(hpb):
        q = q_ref[:, r * DIFF_V_DIM:(r + 1) * DIFF_V_DIM]
        zero = jnp.zeros_like(q)
        streams.append((r, jnp.where(lane < half, q, zero)))
        streams.append((r, jnp.where(lane >= half, q, zero)))

    def update(t0, nk, carry, bias):
        row0 = pl.multiple_of(t0 * tq, tq)
        v0 = t0 * (tq // vt_tile)
        ones = jnp.ones((BF16_ROWS, nk), BF16)
        sts = []
        for r, qc in streams:
            k = k_ref[pl.ds(row0, nk), r * DIFF_V_DIM:(r + 1) * DIFF_V_DIM]
            st = lax.dot_general(k, qc, (((1,), (1,)), ((), ())), preferred_element_type=F32)
            sts.append(st if bias is None else st + bias)
        out = []
        for (r, _), st, (m, acc) in zip(streams, sts, carry):
            m_new = jnp.maximum(m, jnp.max(st, axis=0, keepdims=True))
            alpha = jnp.exp2(m - m_new)
            p = jnp.exp2(st - m_new).astype(BF16)
            va = jnp.concatenate([vt_ref[v0 + t, r * DIFF_V_DIM:(r + 1) * DIFF_V_DIM, :]
                                  for t in range(nk // vt_tile)], axis=1)
            va = jnp.concatenate([va, ones], axis=0)
            out.append((m_new, alpha * acc + jnp.dot(va, p, preferred_element_type=F32)))
        return tuple(out)

    init = tuple((jnp.full((1, tq), NEG_BIG, F32), jnp.zeros((DIFF_V_DIM + BF16_ROWS, tq), F32))
                 for _ in streams)
    per = tk // tq
    carry = lax.fori_loop(0, i // per, lambda j, c: update(j * per, tk, c, None), init)
    carry = lax.cond(i % per == 1, lambda c: update(i - 1, tq, c, None), lambda c: c, carry)
    carry = update(i, tq, carry, bias_ref[...])
    for r in range(hpb):
        a1, a2 = (acc[:DIFF_V_DIM] / acc[DIFF_V_DIM:DIFF_V_DIM + 1] for _, acc in carry[2 * r:2 * r + 2])
        ot = a1 - lam_ref[0] * a2
        ms = jnp.mean(ot * ot, axis=0, keepdims=True)
        ot = ot * lax.rsqrt(ms + EPS)
        o_ref[:, r * DIFF_V_DIM:(r + 1) * DIFF_V_DIM] = (ot.T * g_ref[...] * post_scale).astype(o_ref.dtype)


def attn_prompt(q, k, vt, lam, g, seq, tq, tk, hpb, post_scale):
    assert tk == 2 * tq, "full key tiles are two query tiles long"
    vt_tile = vt.shape[2]
    kern = functools.partial(_attn_prompt_kernel, tq=tq, tk=tk, vt_tile=vt_tile, hpb=hpb,
                             post_scale=post_scale)
    kk = np.arange(tq)[:, None]
    qq = np.arange(tq)[None, :]
    bias = jnp.asarray(np.where(kk <= qq, 0.0, NEG_BIG).astype(np.float32))
    w = hpb * DIFF_V_DIM
    return pl.pallas_call(
        kern,
        grid=(DIFF_HEADS // hpb, seq // tq),
        in_specs=[
            pl.BlockSpec(memory_space=pltpu.SMEM),
            pl.BlockSpec((tq, w), lambda h, i: (i, h)),
            pl.BlockSpec((seq, w), lambda h, i: (0, h)),
            pl.BlockSpec((seq // vt_tile, w, vt_tile), lambda h, i: (0, h, 0)),
            pl.BlockSpec((1, DIFF_V_DIM), lambda h, i: (0, 0)),
            pl.BlockSpec((tq, tq), lambda h, i: (0, 0)),
        ],
        out_specs=pl.BlockSpec((tq, w), lambda h, i: (i, h)),
        out_shape=jax.ShapeDtypeStruct((seq, DIFF_W), BF16),
        compiler_params=_cparams(("arbitrary", "arbitrary")),
        name="attn_prompt",
    )(lam.reshape(1), q, k, vt, g.reshape(1, DIFF_V_DIM), bias)


def _attn_sample_kernel(pt_ref, lam_ref, wq_ref, kn_ref, vn_ref, g_ref, bias_ref, nbias_ref, *rest,
                        pages, group, n_steps, n_new, post_scale):
    k_refs = rest[:pages]
    v_refs = rest[pages:2 * pages]
    o_ref = rest[2 * pages]
    m_scr, l_scr, acc_scr = rest[2 * pages + 1:]
    step = pl.program_id(1)
    cols = PAGE_SIZE * DIFF_HEADS

    @pl.when(step == 0)
    def _():
        m_scr[...] = jnp.full(m_scr.shape, NEG_BIG, F32)
        l_scr[...] = jnp.zeros(l_scr.shape, F32)
        acc_scr[...] = jnp.zeros(acc_scr.shape, F32)

    wq = wq_ref[0]

    def online(s_list, v_list):
        m_old = m_scr[...]
        m_new = m_old
        for s in s_list:
            m_new = jnp.maximum(m_new, jnp.max(s, axis=-1, keepdims=True))
        alpha = jnp.exp2(m_old - m_new)
        l = alpha * l_scr[...]
        acc = alpha * acc_scr[...]
        for s, v in zip(s_list, v_list):
            p = jnp.exp2(s - m_new)
            l = l + jnp.sum(p, axis=-1, keepdims=True)
            acc = acc + jnp.dot(p.astype(BF16), v, preferred_element_type=F32)
        m_scr[...] = m_new
        l_scr[...] = l
        acc_scr[...] = acc

    bias = bias_ref[...]

    def scores(g):
        k2 = k_refs[g][...].reshape(cols, DIFF_V_DIM).astype(BF16)
        s = lax.dot_general(wq, k2, (((1,), (1,)), ((), ())), preferred_element_type=F32)
        return s + bias

    def values(g):
        return v_refs[g][...].reshape(cols, DIFF_V_DIM).astype(BF16)

    groups = [list(range(g0, g0 + group)) for g0 in range(0, pages, group)]
    s_next = [scores(g) for g in groups[0]]
    for gi, grp in enumerate(groups):
        s_cur = s_next
        if gi + 1 < len(groups):
            s_next = [scores(g) for g in groups[gi + 1]]
        online(s_cur, [values(g) for g in grp])

    @pl.when(step == n_steps - 1)
    def _():
        s = lax.dot_general(wq, kn_ref[0], (((1,), (1,)), ((), ())), preferred_element_type=F32)
        online([s + nbias_ref[...]], [vn_ref[0]])
        a = acc_scr[...] / l_scr[...]
        lam = lam_ref[0]
        outs = []
        for h in range(DIFF_HEADS):
            r0 = h * 2 * n_new
            o = a[r0:r0 + n_new] - lam * a[r0 + n_new:r0 + 2 * n_new]
            ms = jnp.mean(o * o, axis=-1, keepdims=True)
            outs.append(o * lax.rsqrt(ms + EPS) * g_ref[...] * post_scale)
        o_ref[0] = jnp.concatenate(outs, axis=-1)


def _sample_biases(n_new):
    rows = 2 * DIFF_HEADS * n_new
    r = np.arange(rows)[:, None]
    c = np.arange(PAGE_SIZE * DIFF_HEADS)[None, :]
    bias = np.where((r // (2 * n_new)) == (c % DIFF_HEADS), 0.0, NEG_BIG)
    c = np.arange(LANES)[None, :]
    ok = ((r // (2 * n_new)) == (c % DIFF_HEADS)) & ((c // DIFF_HEADS) <= (r % n_new)) \
        & (c < n_new * DIFF_HEADS)
    return jnp.asarray(bias, F32), jnp.asarray(np.where(ok, 0.0, NEG_BIG), F32)


def attn_sample(wq, k_new, v_new, lam, g, cache_k, cache_v, page_table, layer, pages, post_scale):
    batch, n_pages = page_table.shape
    n_new = wq.shape[1] // (2 * DIFF_HEADS)
    n_steps = n_pages // pages
    kern = functools.partial(_attn_sample_kernel, pages=pages, group=PAGE_GROUP, n_steps=n_steps,
                             n_new=n_new, post_scale=post_scale)
    bias, nbias = _sample_biases(n_new)

    def page_spec(gi):
        return pl.BlockSpec((None, None, PAGE_SIZE, DIFF_HEADS, DIFF_V_DIM),
                            lambda b, p, pt: (layer, pt[b, p * pages + gi], 0, 0, 0))

    per_b = lambda b, p, pt: (b, 0, 0)
    c2 = lambda b, p, pt: (0, 0)
    rows = wq.shape[1]
    grid_spec = pltpu.PrefetchScalarGridSpec(
        num_scalar_prefetch=1,
        grid=(batch, n_steps),
        in_specs=[
            pl.BlockSpec(memory_space=pltpu.SMEM),
            pl.BlockSpec((1, rows, DIFF_V_DIM), per_b),
            pl.BlockSpec((1,) + k_new.shape[1:], per_b),
            pl.BlockSpec((1,) + v_new.shape[1:], per_b),
            pl.BlockSpec((1, DIFF_V_DIM), c2),
            pl.BlockSpec(bias.shape, c2),
            pl.BlockSpec(nbias.shape, c2),
        ] + [page_spec(gi) for gi in range(pages)] * 2,
        out_specs=pl.BlockSpec((1, n_new, DIFF_W), per_b),
        scratch_shapes=[
            pltpu.VMEM((rows, 1), F32),
            pltpu.VMEM((rows, 1), F32),
            pltpu.VMEM((rows, DIFF_V_DIM), F32),
        ],
    )
    return pl.pallas_call(
        kern,
        grid_spec=grid_spec,
        out_shape=jax.ShapeDtypeStruct((batch, n_new, DIFF_W), F32),
        compiler_params=_cparams(("arbitrary", "arbitrary")),
        name="attn_sample",
    )(page_table, lam.reshape(1), wq, k_new, v_new, g.reshape(1, DIFF_V_DIM), bias, nbias,
      *([cache_k] * pages), *([cache_v] * pages))


def _merge_kernel(x_ref, ro_ref, do_ref, lo_ref, gz_ref, wb_ref, wo_ref, y_ref):
    mix = None
    for n, br in enumerate((ro_ref, do_ref, lo_ref)):
        proj = jnp.dot(br[...].astype(BF16), wb_ref[n], preferred_element_type=F32)
        gz = gz_ref[:, n * D_MODEL:(n + 1) * D_MODEL].astype(F32)
        term = jax.nn.sigmoid(gz) * proj
        mix = term if mix is None else mix + term
    y_ref[...] = x_ref[...] + jnp.dot(mix.astype(BF16), wo_ref[...], preferred_element_type=F32)


def merge(x, ro, do, lo, gz, wb_bf, wo_bf, tm):
    m = x.shape[0]
    row = lambda i: (i, 0)
    return pl.pallas_call(
        _merge_kernel,
        grid=(m // tm,),
        in_specs=[
            pl.BlockSpec((tm, D_MODEL), row),
            pl.BlockSpec((tm, D_MODEL), row),
            pl.BlockSpec((tm, D_MODEL), row),
            pl.BlockSpec((tm, D_MODEL), row),
            pl.BlockSpec((tm, N_BRANCH * D_MODEL), row),
            _resident((N_BRANCH, D_MODEL, D_MODEL), lambda i: (0, 0, 0)),
            _resident((D_MODEL, D_MODEL), lambda i: (0, 0)),
        ],
        out_specs=pl.BlockSpec((tm, D_MODEL), row),
        out_shape=jax.ShapeDtypeStruct((m, D_MODEL), F32),
        compiler_params=_cparams(("arbitrary",)),
        name="merge",
    )(x, ro, do, lo, gz, wb_bf, wo_bf)


FF_CHUNK = D_FF // 2


def _ffn_kernel(x_ref, g_ref, wg_ref, wu_ref, wd_ref, fg_ref, y_ref, *, final_norm):
    x = x_ref[...]
    ms = jnp.mean(x * x, axis=-1, keepdims=True)
    h = (x * lax.rsqrt(ms + EPS) * g_ref[...]).astype(BF16)
    y = x
    for c in range(D_FF // FF_CHUNK):
        cs = slice(c * FF_CHUNK, (c + 1) * FF_CHUNK)
        gate = jnp.dot(h, wg_ref[:, cs], preferred_element_type=F32)
        up = jnp.dot(h, wu_ref[:, cs], preferred_element_type=F32)
        act = (gate * jax.nn.sigmoid(gate) * up).astype(BF16)
        y = y + jnp.dot(act, wd_ref[cs, :], preferred_element_type=F32)
    if final_norm:
        ms = jnp.mean(y * y, axis=-1, keepdims=True)
        y = y * lax.rsqrt(ms + EPS) * fg_ref[...]
    y_ref[...] = y


def ffn(x, g, wg_bf, wu_bf, wd_bf, final_g, tm, final_norm):
    m = x.shape[0]
    c2 = lambda i: (0, 0)
    return pl.pallas_call(
        functools.partial(_ffn_kernel, final_norm=final_norm),
        grid=(m // tm,),
        in_specs=[
            pl.BlockSpec((tm, D_MODEL), lambda i: (i, 0)),
            pl.BlockSpec((1, D_MODEL), c2),
            _resident((D_MODEL, D_FF), c2),
            _resident((D_MODEL, D_FF), c2),
            _resident((D_FF, D_MODEL), c2),
            pl.BlockSpec((1, D_MODEL), c2),
        ],
        out_specs=pl.BlockSpec((tm, D_MODEL), lambda i: (i, 0)),
        out_shape=jax.ShapeDtypeStruct((m, D_MODEL), F32),
        compiler_params=_cparams(("arbitrary",)),
        name="ffn",
    )(x, g.reshape(1, D_MODEL), wg_bf, wu_bf, wd_bf, final_g.reshape(1, D_MODEL))


def _sample_query_blocks(dq, n_new):
    b = dq.shape[0] // n_new
    q = dq.reshape(b, n_new, DIFF_HEADS, 2, DIFF_HEAD_DIM).transpose(0, 2, 3, 1, 4)
    eye = jnp.eye(2, dtype=q.dtype)
    blk = q[:, :, :, :, None, :] * eye[None, None, :, None, :, None]
    return blk.reshape(b, DIFF_HEADS * 2 * n_new, 2 * DIFF_HEAD_DIM)


def _new_token_rows(x, n_new):
    b = x.shape[0] // n_new
    r = x.reshape(b, n_new * DIFF_HEADS, DIFF_V_DIM)
    return jnp.pad(r, ((0, 0), (0, LANES - n_new * DIFF_HEADS), (0, 0)))


def kernel(x_prompt, x_sample, cache_k, cache_v, page_table, state_ret, state_conv, state_lru,
           attn_norm_g, w_in, ret_gn_g, diff_lambda, diff_ln_g, conv_w, conv_b,
           gate_a_w, gate_a_b, gate_x_w, gate_x_b, lru_lambda, w_branch, w_out,
           ffn_norm_g, w_gate, w_up, w_down, final_norm_g):
    bp, seq, _ = x_prompt.shape
    bs, n_new, _ = x_sample.shape
    depth = w_in.shape[0]
    past = page_table.shape[1] * PAGE_SIZE
    pos_p = jnp.arange(seq)
    pos_s = past + jnp.arange(n_new)

    xp = x_prompt.reshape(bp * seq, D_MODEL)
    xs = x_sample.reshape(bs * n_new, D_MODEL)
    ret0 = jnp.zeros((bp, RET_HEADS, RET_QK_DIM, RET_V_DIM), F32)
    conv0 = jnp.zeros((bp, CONV_W - 1, LRU_W), F32)
    h00 = jnp.zeros((bp, LRU_W), F32)

    tm_s = bs * n_new
    heads = lambda a, b, n: a.reshape(depth, b, n, DIFF_HEADS, DIFF_V_DIM)
    outs = {k: [] for k in ("cp", "hp", "cs", "hs")}
    for l in range(depth):
        lam_init = 0.8 - 0.6 * math.exp(-0.3 * l)
        post = 1.0 - lam_init
        lq1, lk1, lq2, lk2 = diff_lambda[l].astype(F32)
        lam = jnp.exp(jnp.sum(lq1 * lk1)) - jnp.exp(jnp.sum(lq2 * lk2)) + lam_init
        w_in_bf = w_in[l].astype(BF16)
        wb_bf = w_branch[l].astype(BF16)
        wo_bf = w_out[l].astype(BF16)
        wg_bf = w_gate[l].astype(BF16)
        wu_bf = w_up[l].astype(BF16)
        wd_bf = w_down[l].astype(BF16)
        wgate_bf = jnp.concatenate([gate_a_w[l], gate_x_w[l]], axis=-1).astype(BF16)
        last = l == depth - 1

        zr, q, k, vt, kp_all, vp_all, lx, gz = in_proj(xp, attn_norm_g[l], w_in_bf, TM_PROJ, True, BF16, l, depth,
                                                       () if l == 0 else (kp_all, vp_all))
        ro, rp_all = retention(zr, pos_p, ret0, ret_gn_g[l], bp, seq, RET_CHUNK, RET_CHUNK, BF16, l, depth,
                               () if l == 0 else (rp_all,))
        do = attn_prompt(q, k, vt, lam, diff_ln_g[l], seq, ATT_TQ, ATT_TK, ATT_HEADS, post)
        lo, c1, h1 = conv_lru(lx, conv0, h00, conv_w[l], conv_b[l], wgate_bf, gate_a_b[l],
                              gate_x_b[l], lru_lambda[l], bp, seq, LRU_CHUNK, BF16)
        xp = merge(xp, ro, do, lo, gz, wb_bf, wo_bf, TM_DENSE)
        xp = ffn(xp, ffn_norm_g[l], wg_bf, wu_bf, wd_bf, final_norm_g, TM_DENSE, last)
        outs["cp"].append(c1)
        outs["hp"].append(h1.reshape(bp, LRU_W))

        zr, q, k, v, ks_all, vs_all, lx, gz = in_proj(xs, attn_norm_g[l], w_in_bf, tm_s, False, F32, l, depth,
                                                      () if l == 0 else (ks_all, vs_all))
        ro, rs_all = retention(zr, pos_s, state_ret[l], ret_gn_g[l], bs, n_new, n_new, LANES, F32, l, depth,
                               () if l == 0 else (rs_all,))
        wq = _sample_query_blocks(q, n_new)
        do = attn_sample(wq, _new_token_rows(k, n_new), _new_token_rows(v, n_new), lam, diff_ln_g[l],
                         cache_k, cache_v, page_table, l, PAGES_PER_STEP, post)
        do = do.reshape(bs * n_new, DIFF_W)
        lo, c2, h2 = conv_lru(lx, state_conv[l], state_lru[l], conv_w[l], conv_b[l], wgate_bf,
                              gate_a_b[l], gate_x_b[l], lru_lambda[l], bs, n_new, n_new, F32)
        xs = merge(xs, ro, do, lo, gz, wb_bf, wo_bf, tm_s)
        xs = ffn(xs, ffn_norm_g[l], wg_bf, wu_bf, wd_bf, final_norm_g, tm_s, last)
        outs["cs"].append(c2)
        outs["hs"].append(h2.reshape(bs, LRU_W))

    st = lambda k: jnp.stack(outs[k])
    return (xp.reshape(bp, seq, D_MODEL), xs.reshape(bs, n_new, D_MODEL),
            heads(kp_all, bp, seq), heads(vp_all, bp, seq), rp_all, st("cp"), st("hp"),
            heads(ks_all, bs, n_new), heads(vs_all, bs, n_new), rs_all, st("cs"), st("hs"))
```
